```python
import jax, jax.numpy as jnp
from jax import lax
import numpy as np

D_MODEL = 2048
BATCH = 4
SEQ = 8192
DEPTH = 4

N_MIXERS = 2
GRID_W = 64
EPS = 1e-6
ADA_STD = 0.5

NA_HEADS = 64
NA_HEAD_DIM = D_MODEL // NA_HEADS
NA_WIN_ROWS = 8
NA_WIN_COLS = 16

GLA_HEADS = 4
GLA_DK = D_MODEL // 2
GLA_DV = D_MODEL
GLA_HK = GLA_DK // GLA_HEADS
GLA_HV = GLA_DV // GLA_HEADS
GLA_GATE_RANK = 16
GLA_GATE_NORM = 16.0
GLA_CHUNK = 64
GLA_IN = 2 * GLA_DK + 2 * GLA_DV + 2 * GLA_GATE_RANK

MOE_GROUPS = 4
MOE_EXPERTS_PER_GROUP = 8
MOE_EXPERTS = MOE_GROUPS * MOE_EXPERTS_PER_GROUP
MOE_TOP_K = 2
MOE_FF = 512
MOE_BLOCK = 128

kernel_name = "hybrid_na_gla_hmoe_encoder"


def rms_norm(x, g):
    x32 = x.astype(jnp.float32)
    y = x32 * lax.rsqrt(jnp.mean(x32 * x32, axis=-1, keepdims=True) + EPS)
    return (y * g.astype(jnp.float32)).astype(x.dtype)


def neighbourhood_attention(h, w_in, w_out, rpb):
    b, s, d = h.shape
    rows = s // GRID_W
    wh = min(NA_WIN_ROWS, rows)
    ww = NA_WIN_COLS
    qkv = (h @ w_in).reshape(b, rows, GRID_W, 3, NA_HEADS, NA_HEAD_DIM)
    q = jnp.moveaxis(qkv[:, :, :, 0], 1, 0)
    k = jnp.moveaxis(qkv[:, :, :, 1], 1, 0)
    v = jnp.moveaxis(qkv[:, :, :, 2], 1, 0)
    cols = np.arange(GRID_W)
    col_start = np.clip(cols - ww // 2, 0, GRID_W - ww)
    col_idx = col_start[:, None] + np.arange(ww)[None, :]
    col_off = col_idx - cols[:, None] + (NA_WIN_COLS - 1)
    rpb_cols = rpb[:, :, col_off]
    scale = NA_HEAD_DIM ** -0.5

    def row_block(args):
        q_r, r = args
        rs = jnp.clip(r - wh // 2, 0, rows - wh)
        k_rows = lax.dynamic_slice_in_dim(k, rs, wh, axis=0)
        v_rows = lax.dynamic_slice_in_dim(v, rs, wh, axis=0)
        k_g = k_rows[:, :, col_idx]
        v_g = v_rows[:, :, col_idx]
        logits = jnp.einsum('bqhd,ibqjhd->bhqij', q_r, k_g).astype(jnp.float32) * scale
        row_off = rs + jnp.arange(wh) - r + (NA_WIN_ROWS - 1)
        bias = jnp.transpose(rpb_cols[:, row_off], (0, 2, 1, 3))
        logits = logits + bias[None].astype(jnp.float32)
        p = jax.nn.softmax(logits.reshape(b, NA_HEADS, GRID_W, wh * ww), axis=-1)
        p = p.reshape(b, NA_HEADS, GRID_W, wh, ww).astype(h.dtype)
        return jnp.einsum('bhqij,ibqjhd->bqhd', p, v_g)

    out = lax.map(row_block, (q, jnp.arange(rows, dtype=jnp.int32)))
    out = jnp.moveaxis(out, 0, 1).reshape(b, s, d)
    return out @ w_out


def gla_direction(q, k, v, log_a):
    b, nh, s, _ = q.shape
    n = s // GLA_CHUNK

    def to_chunks(t):
        return jnp.moveaxis(t.reshape(b, nh, n, GLA_CHUNK, t.shape[-1]), 2, 0)

    mask = jnp.tril(jnp.ones((GLA_CHUNK, GLA_CHUNK), dtype=bool))

    def step(state, inp):
        qc, kc, vc, gc = inp
        cum = jnp.cumsum(gc, axis=2)
        last = cum[:, :, -1:, :]
        qe = qc * jnp.exp(cum)
        ke = kc * jnp.exp(-cum)
        att = jnp.where(mask, jnp.einsum('bhck,bhsk->bhcs', qe, ke), 0.0)
        out = att @ vc + jnp.einsum('bhck,bhkv->bhcv', qe, state)
        kd = kc * jnp.exp(last - cum)
        state = jnp.exp(last[:, :, 0, :])[..., None] * state + jnp.einsum('bhck,bhcv->bhkv', kd, vc)
        return state, out

    s0 = jnp.zeros((b, nh, q.shape[-1], v.shape[-1]), jnp.float32)
    _, out = lax.scan(step, s0, (to_chunks(q), to_chunks(k), to_chunks(v), to_chunks(log_a)))
    return jnp.moveaxis(out, 0, 2).reshape(b, nh, s, v.shape[-1])


def gated_linear_attention(h, w_in, w_gate_up, b_gate, gn_g, w_out):
    b, s, d = h.shape
    proj = h @ w_in
    q, k, v, r, a_low = jnp.split(
        proj, [GLA_DK, 2 * GLA_DK, 2 * GLA_DK + GLA_DV, 2 * GLA_DK + 2 * GLA_DV], axis=-1)
    a_low = a_low.reshape(b, s, 2, GLA_GATE_RANK)
    gate_logits = jnp.einsum('bsir,irk->ibsk', a_low, w_gate_up) + b_gate[:, None, None, :]
    log_a = jax.nn.log_sigmoid(gate_logits.astype(jnp.float32)) / GLA_GATE_NORM

    def heads(t, hd):
        return jnp.transpose(t.reshape(b, s, GLA_HEADS, hd), (0, 2, 1, 3)).astype(jnp.float32)

    qh = heads(q, GLA_HK) * (GLA_HK ** -0.5)
    kh = heads(k, GLA_HK)
    vh = heads(v, GLA_HV)
    la_f = heads(log_a[0], GLA_HK)
    la_b = heads(log_a[1], GLA_HK)
    flip = lambda t: jnp.flip(t, axis=2)
    o = gla_direction(qh, kh, vh, la_f) + flip(gla_direction(flip(qh), flip(kh), flip(vh), flip(la_b)))
    o = o * lax.rsqrt(jnp.mean(o * o, axis=-1, keepdims=True) + EPS)
    o = o * gn_g.astype(jnp.float32).reshape(GLA_HEADS, GLA_HV)[None, :, None, :]
    o = jnp.transpose(o, (0, 2, 1, 3)).reshape(b, s, GLA_DV)
    o = (o * jax.nn.silu(r.astype(jnp.float32))).astype(h.dtype)
    return o @ w_out


def hierarchical_moe(h, w_router_group, w_router_expert, w_up, w_down):
    b, s, d = h.shape
    n = b * s
    xt = h.reshape(n, d)
    g_logits = (xt @ w_router_group).astype(jnp.float32)
    g_prob = jax.nn.softmax(g_logits, axis=-1)
    g_idx = jnp.argmax(g_logits, axis=-1).astype(jnp.int32)
    g_w = jnp.take_along_axis(g_prob, g_idx[:, None], axis=-1)
    e_logits = (xt @ w_router_expert).astype(jnp.float32).reshape(n, MOE_GROUPS, MOE_EXPERTS_PER_GROUP)
    e_logits = jnp.take_along_axis(e_logits, g_idx[:, None, None], axis=1)[:, 0]
    e_prob = jax.nn.softmax(e_logits, axis=-1)
    top_p, top_i = lax.top_k(e_prob, MOE_TOP_K)
    weights = g_w * top_p / jnp.sum(top_p, axis=-1, keepdims=True)
    expert_id = g_idx[:, None] * MOE_EXPERTS_PER_GROUP + top_i.astype(jnp.int32)

    a = n * MOE_TOP_K
    flat_e = expert_id.reshape(a)
    flat_tok = jnp.arange(a, dtype=jnp.int32) // MOE_TOP_K
    flat_w = weights.reshape(a)
    order = jnp.argsort(flat_e)
    e_sorted = flat_e[order]
    tok_sorted = flat_tok[order]
    w_sorted = flat_w[order]
    counts = jnp.zeros((MOE_EXPERTS,), jnp.int32).at[flat_e].add(1)
    padded = (counts + MOE_BLOCK - 1) // MOE_BLOCK * MOE_BLOCK
    padded_end = jnp.cumsum(padded)
    padded_start = padded_end - padded
    start = jnp.cumsum(counts) - counts
    dest = padded_start[e_sorted] + jnp.arange(a, dtype=jnp.int32) - start[e_sorted]
    n_slots = a + MOE_EXPERTS * MOE_BLOCK
    n_blocks = n_slots // MOE_BLOCK
    slot_tok = jnp.zeros((n_slots,), jnp.int32).at[dest].set(tok_sorted)
    block_start = jnp.arange(n_blocks, dtype=jnp.int32) * MOE_BLOCK
    block_expert = jnp.minimum(jnp.searchsorted(padded_end, block_start, side='right'),
                               MOE_EXPERTS - 1).astype(jnp.int32)
    xb = xt[slot_tok].reshape(n_blocks, MOE_BLOCK, d)

    def expert_block(args):
        x_blk, e = args
        gate, up = jnp.split(x_blk @ w_up[e], 2, axis=-1)
        return (jax.nn.silu(gate) * up) @ w_down[e]

    yb = lax.map(expert_block, (xb, block_expert)).reshape(n_slots, d)
    y = yb[dest] * w_sorted[:, None].astype(yb.dtype)
    out = jax.ops.segment_sum(y, tok_sorted, num_segments=n)
    return out.reshape(b, s, d)


def setup_inputs(seed: int = 0) -> dict:
    key = jax.random.key(seed)
    ks = jax.random.split(key, 22)
    d = D_MODEL
    n_na = len(range(0, DEPTH, N_MIXERS))
    n_gla = DEPTH - n_na

    def nrm(k, shape, std):
        return jax.random.normal(k, shape, jnp.float32) * std

    return {
        "x": nrm(ks[0], (BATCH, SEQ, d), 1.0),
        "c": nrm(ks[1], (BATCH, d), 1.0),
        "ada_w": nrm(ks[2], (DEPTH, d, 6 * d), ADA_STD * d ** -0.5),
        "ada_b": nrm(ks[3], (DEPTH, 6 * d), 0.01),
        "norm1_g": 1.0 + nrm(ks[4], (DEPTH, d), 0.02),
        "norm2_g": 1.0 + nrm(ks[5], (DEPTH, d), 0.02),
        "na_w_in": nrm(ks[6], (n_na, d, 3 * d), d ** -0.5),
        "na_w_out": nrm(ks[7], (n_na, d, d), d ** -0.5),
        "na_rpb": nrm(ks[8], (n_na, NA_HEADS, 2 * NA_WIN_ROWS - 1, 2 * NA_WIN_COLS - 1), 0.1),
        "gla_w_in": nrm(ks[9], (n_gla, d, GLA_IN), d ** -0.5),
        "gla_w_gate_up": nrm(ks[10], (n_gla, 2, GLA_GATE_RANK, GLA_DK), GLA_GATE_RANK ** -0.5),
        "gla_b_gate": nrm(ks[11], (n_gla, 2, GLA_DK), 0.1),
        "gla_gn_g": 1.0 + nrm(ks[12], (n_gla, GLA_DV), 0.02),
        "gla_w_out": nrm(ks[13], (n_gla, GLA_DV, d), GLA_DV ** -0.5),
        "moe_w_router_group": nrm(ks[14], (DEPTH, d, MOE_GROUPS), d ** -0.5),
        "moe_w_router_expert": nrm(ks[15], (DEPTH, d, MOE_EXPERTS), d ** -0.5),
        "moe_w_up": nrm(ks[16], (DEPTH, MOE_EXPERTS, d, 2 * MOE_FF), d ** -0.5),
        "moe_w_down": nrm(ks[17], (DEPTH, MOE_EXPERTS, MOE_FF, d), MOE_FF ** -0.5),
        "final_g": 1.0 + nrm(ks[18], (d,), 0.02),
    }


def reference(x, c, ada_w, ada_b, norm1_g, norm2_g, na_w_in, na_w_out, na_rpb,
              gla_w_in, gla_w_gate_up, gla_b_gate, gla_gn_g, gla_w_out,
              moe_w_router_group, moe_w_router_expert, moe_w_up, moe_w_down, final_g):
    cond = jax.nn.silu(c)
    i_na = 0
    i_gla = 0
    for i in range(DEPTH):
        mod = cond @ ada_w[i] + ada_b[i]
        sh1, sc1, gt1, sh2, sc2, gt2 = [m[:, None, :] for m in jnp.split(mod, 6, axis=-1)]
        h = rms_norm(x, norm1_g[i]) * (1.0 + sc1) + sh1
        if i % N_MIXERS == 0:
            y = neighbourhood_attention(h, na_w_in[i_na], na_w_out[i_na], na_rpb[i_na])
            i_na += 1
        else:
            y = gated_linear_attention(h, gla_w_in[i_gla], gla_w_gate_up[i_gla], gla_b_gate[i_gla],
                                       gla_gn_g[i_gla], gla_w_out[i_gla])
            i_gla += 1
        x = x + gt1 * y
        h = rms_norm(x, norm2_g[i]) * (1.0 + sc2) + sh2
        x = x + gt2 * hierarchical_moe(h, moe_w_router_group[i], moe_w_router_expert[i],
                                       moe_w_up[i], moe_w_down[i])
    return rms_norm(x, final_g)
```

```python
import functools

import jax
import jax.numpy as jnp
import numpy as np
from jax import lax
from jax.experimental import pallas as pl
from jax.experimental.pallas import tpu as pltpu

F32 = jnp.float32
BF16 = jnp.bfloat16

EPS = 1e-6
GRID_W = 64
NA_HEAD_DIM = 32
NA_WIN_ROWS = 8
NA_WIN_COLS = 16
NA_HEADS_PER_CHUNK = 4
NA_QBLOCKS = ((0, 24, 0), (24, 16, 16), (40, 24, 32))
NA_KEY_COLS = 32
NA_ROWS_PER_STEP = 8
MASK_VALUE = -1e30

GLA_HEADS = 4
GLA_GATE_RANK = 16
GLA_GATE_NORM = 16.0
GLA_CHUNK = 64
GLA_CHUNKS_PER_STEP = 8

MOE_GROUPS = 4
MOE_EXPERTS_PER_GROUP = 8
MOE_EXPERTS = MOE_GROUPS * MOE_EXPERTS_PER_GROUP
MOE_BLOCK = 256
ROUTER_LANES = 128

VMEM_LIMIT = 56 * 1024 * 1024


def _cparams(sem):
    return pltpu.CompilerParams(dimension_semantics=sem, vmem_limit_bytes=VMEM_LIMIT)


def _silu(v):
    return v * jax.nn.sigmoid(v)


def _adaln_kernel(c_ref, w_ref, b_ref, o_ref):
    cond = _silu(c_ref[...]).astype(BF16)
    w = w_ref[...].astype(BF16)
    o_ref[...] = jnp.dot(cond, w, preferred_element_type=F32) + b_ref[...]


def _adaln(c, ada_w, ada_b):
    depth, d, n6 = ada_w.shape
    b = c.shape[0]
    rows = 16
    cpad = jnp.zeros((rows, d), F32).at[:b].set(c)
    tn = 1024
    out = pl.pallas_call(
        _adaln_kernel,
        grid=(depth, n6 // tn),
        in_specs=[
            pl.BlockSpec((rows, d), lambda i, j: (0, 0)),
            pl.BlockSpec((None, d, tn), lambda i, j: (i, 0, j)),
            pl.BlockSpec((None, 1, tn), lambda i, j: (i, 0, j)),
        ],
        out_specs=pl.BlockSpec((None, rows, tn), lambda i, j: (i, 0, j)),
        out_shape=jax.ShapeDtypeStruct((depth, rows, n6), F32),
        compiler_params=_cparams(("arbitrary", "arbitrary")),
        name="adaln",
    )(cpad, ada_w, ada_b.reshape(depth, 1, n6))
    return out[:, :b]


def _norm_mod(x, g, sc, sh):
    ms = jnp.mean(x * x, axis=-1, keepdims=True)
    y = x * lax.rsqrt(ms + EPS) * g
    return y * (1.0 + sc) + sh


def _nmm_kernel(x_ref, g_ref, sc_ref, sh_ref, w_ref, cs_ref, o_ref, h_ref):
    @pl.when(pl.program_id(1) == 0)
    def _():
        h_ref[...] = _norm_mod(x_ref[...], g_ref[...], sc_ref[...], sh_ref[...]).astype(BF16)

    acc = jnp.dot(h_ref[...], w_ref[...], preferred_element_type=F32)
    o_ref[...] = (acc * cs_ref[...]).astype(o_ref.dtype)


def _norm_mod_matmul(x, g, sc, sh, w, colscale, seq, tn):
    n, d = x.shape
    nout = w.shape[1]
    tm = min(1024, seq)
    return pl.pallas_call(
        _nmm_kernel,
        grid=(n // tm, nout // tn),
        in_specs=[
            pl.BlockSpec((tm, d), lambda i, j: (i, 0)),
            pl.BlockSpec((1, d), lambda i, j: (0, 0)),
            pl.BlockSpec((None, 1, d), lambda i, j: (i * tm // seq, 0, 0)),
            pl.BlockSpec((None, 1, d), lambda i, j: (i * tm // seq, 0, 0)),
            pl.BlockSpec((d, tn), lambda i, j: (0, j)),
            pl.BlockSpec((1, tn), lambda i, j: (0, j)),
        ],
        out_specs=pl.BlockSpec((tm, tn), lambda i, j: (i, j)),
        out_shape=jax.ShapeDtypeStruct((n, nout), BF16),
        scratch_shapes=[pltpu.VMEM((tm, d), BF16)],
        compiler_params=_cparams(("arbitrary", "arbitrary")),
        name="norm_mod_matmul",
    )(x, g.reshape(1, d), sc, sh, w, colscale.reshape(1, nout))


def _mmres_kernel(a_ref, w_ref, res_ref, gate_ref, o_ref):
    acc = jnp.dot(a_ref[...], w_ref[...], preferred_element_type=F32)
    o_ref[...] = res_ref[...] + gate_ref[...] * acc


def _matmul_residual(a, w, res, gate, seq):
    n, k = a.shape
    d = w.shape[1]
    tm = min(512, seq)
    return pl.pallas_call(
        _mmres_kernel,
        grid=(n // tm,),
        in_specs=[
            pl.BlockSpec((tm, k), lambda i: (i, 0)),
            pl.BlockSpec((k, d), lambda i: (0, 0)),
            pl.BlockSpec((tm, d), lambda i: (i, 0)),
            pl.BlockSpec((None, 1, d), lambda i: (i * tm // seq, 0, 0)),
        ],
        out_specs=pl.BlockSpec((tm, d), lambda i: (i, 0)),
        out_shape=jax.ShapeDtypeStruct((n, d), F32),
        compiler_params=_cparams(("arbitrary",)),
        name="matmul_residual",
    )(a, w, res, gate)


def _na_bias_table(rpb):
    heads = rpb.shape[0]
    chunks = heads // NA_HEADS_PER_CHUNK
    hp_l, qcol_l, ks_l = [], [], []
    for qs, nq, ks in NA_QBLOCKS:
        for hp in range(NA_HEADS_PER_CHUNK):
            for ql in range(nq):
                hp_l.append(hp)
                qcol_l.append(qs + ql)
                ks_l.append(ks)
    hp_a = np.array(hp_l)
    qcol = np.array(qcol_l)
    ks_a = np.array(ks_l)
    nrow = hp_a.shape[0]
    kk = np.arange(NA_WIN_ROWS * NA_KEY_COLS)
    win_i = kk // NA_KEY_COLS
    kcol = ks_a[:, None] + (kk % NA_KEY_COLS)[None, :]
    cstart = np.clip(qcol - NA_WIN_COLS // 2, 0, GRID_W - NA_WIN_COLS)
    valid = (kcol >= cstart[:, None]) & (kcol < cstart[:, None] + NA_WIN_COLS)
    col_idx = np.clip(kcol - qcol[:, None] + (NA_WIN_COLS - 1), 0, 2 * NA_WIN_COLS - 2)
    di = np.arange(NA_WIN_ROWS)
    row_idx = win_i[None, :] - di[:, None] + (NA_WIN_ROWS - 1)
    head = np.arange(chunks)[:, None] * NA_HEADS_PER_CHUNK + hp_a[None, :]
    tab = rpb[head[:, None, :, None], row_idx[None, :, None, :], col_idx[None, None, :, :]]
    return jnp.where(valid[None, None], tab.astype(F32), MASK_VALUE), nrow


def _na_kernel(q_ref, k_ref, v_ref, bias_ref, o_ref, *, n_rows):
    rblk = pl.program_id(2)
    lane = lax.broadcasted_iota(jnp.int32, (1, 128), 1)
    hmask = [(lane // NA_HEAD_DIM == hp).astype(F32) for hp in range(NA_HEADS_PER_CHUNK)]
    win_tokens = NA_WIN_ROWS * GRID_W

    def row_body(rr, carry):
        r = rblk * NA_ROWS_PER_STEP + rr
        rs = jnp.clip(r - NA_WIN_ROWS // 2, 0, n_rows - NA_WIN_ROWS)
        di = r - rs
        q0 = pl.multiple_of(rr * GRID_W, GRID_W)
        w0 = pl.multiple_of(rs * GRID_W, GRID_W)
        qrow = q_ref[pl.ds(q0, GRID_W), :].astype(F32)
        kwin = k_ref[pl.ds(w0, win_tokens), :]
        vwin = v_ref[pl.ds(w0, win_tokens), :]
        outs = []
        row0 = 0
        for qs, nq, ks in NA_QBLOCKS:
            qb = qrow[qs:qs + nq]
            qstack = jnp.concatenate([qb * hmask[hp] for hp in range(NA_HEADS_PER_CHUNK)],
                                     axis=0).astype(BF16)
            kj = jnp.concatenate([kwin[i * GRID_W + ks:i * GRID_W + ks + NA_KEY_COLS]
                                  for i in range(NA_WIN_ROWS)], axis=0)
            vj = jnp.concatenate([vwin[i * GRID_W + ks:i * GRID_W + ks + NA_KEY_COLS]
                                  for i in range(NA_WIN_ROWS)], axis=0)
            nst = NA_HEADS_PER_CHUNK * nq
            logits = lax.dot_general(qstack, kj, (((1,), (1,)), ((), ())),
                                     preferred_element_type=F32)
            logits = logits + bias_ref[di, pl.ds(row0, nst), :]
            m = jnp.max(logits, axis=-1, keepdims=True)
            p = jnp.exp(logits - m)
            s = jnp.sum(p, axis=-1, keepdims=True)
            pv = jnp.dot(p.astype(BF16), vj, preferred_element_type=F32)
            pv = pv * (1.0 / s)
            ob = pv[0:nq] * hmask[0]
            for hp in range(1, NA_HEADS_PER_CHUNK):
                ob = ob + pv[hp * nq:(hp + 1) * nq] * hmask[hp]
            outs.append(ob)
            row0 += nst
        orow = jnp.concatenate(outs, axis=0)
        o_ref[pl.ds(q0, GRID_W), :] = orow.astype(o_ref.dtype)
        return carry

    lax.fori_loop(0, NA_ROWS_PER_STEP, row_body, 0)


def _na_attention(qkv, bias_tab, batch, seq, d):
    n = qkv.shape[0]
    n_rows = seq // GRID_W
    chunks = d // 128
    step_tokens = NA_ROWS_PER_STEP * GRID_W
    rsteps = n_rows // NA_ROWS_PER_STEP
    nrow = bias_tab.shape[2]
    nkey = bias_tab.shape[3]
    return pl.pallas_call(
        functools.partial(_na_kernel, n_rows=n_rows),
        grid=(batch, chunks, rsteps),
        in_specs=[
            pl.BlockSpec((step_tokens, 128), lambda b, c, r: (b * rsteps + r, c)),
            pl.BlockSpec((seq, 128), lambda b, c, r: (b, chunks + c)),
            pl.BlockSpec((seq, 128), lambda b, c, r: (b, 2 * chunks + c)),
            pl.BlockSpec((None, NA_WIN_ROWS, nrow, nkey), lambda b, c, r: (c, 0, 0, 0)),
        ],
        out_specs=pl.BlockSpec((step_tokens, 128), lambda b, c, r: (b * rsteps + r, c)),
        out_shape=jax.ShapeDtypeStruct((n, d), BF16),
        compiler_params=_cparams(("arbitrary", "arbitrary", "arbitrary")),
        name="na_attention",
    )(qkv, qkv, qkv, bias_tab)


def _log_sigmoid(v):
    return jnp.minimum(v, 0.0) - jnp.log(1.0 + jnp.exp(-jnp.abs(v)))


def _gla_kernel(q_ref, k_ref, v_ref, r_ref, a_ref, wg_ref, bg_ref, gn_ref, o_ref,
                st_ref, of_ref, *, cb, nb):
    p = pl.program_id(2)
    i = pl.program_id(3)
    fwd = p == 0
    step_rows = cb * GLA_CHUNK

    @pl.when(i == 0)
    def _():
        st_ref[...] = jnp.zeros_like(st_ref)

    blk = jnp.where(fwd, i, nb - 1 - i)
    tr = lax.broadcasted_iota(jnp.int32, (GLA_CHUNK, GLA_CHUNK), 0)
    tc = lax.broadcasted_iota(jnp.int32, (GLA_CHUNK, GLA_CHUNK), 1)
    tmask = (tr - tc) * (1 - 2 * p) >= 0
    tmat = tmask.astype(BF16)
    nt = (((1,), (1,)), ((), ()))

    def chunk_body(cc, carry):
        c = jnp.where(fwd, cc, cb - 1 - cc)
        r0 = pl.multiple_of(c * GLA_CHUNK, GLA_CHUNK)
        a = a_ref[pl.ds(r0, GLA_CHUNK), :]
        gl = jnp.dot(a, wg_ref[...], preferred_element_type=F32) + bg_ref[...]
        g = _log_sigmoid(gl) * (1.0 / GLA_GATE_NORM)
        g1 = g.astype(BF16)
        e1 = g - g1.astype(F32)
        g2 = e1.astype(BF16)
        g3 = (e1 - g2.astype(F32)).astype(BF16)
        cum = (jnp.dot(tmat, g1, preferred_element_type=F32)
               + jnp.dot(tmat, g2, preferred_element_type=F32)
               + jnp.dot(tmat, g3, preferred_element_type=F32))
        last = jnp.where(fwd, cum[GLA_CHUNK - 1:GLA_CHUNK], cum[0:1])
        q = q_ref[pl.ds(r0, GLA_CHUNK), :].astype(F32)
        k = k_ref[pl.ds(r0, GLA_CHUNK), :].astype(F32)
        qe = (q * jnp.exp(cum)).astype(BF16)
        ke = (k * jnp.exp(-cum)).astype(BF16)
        kd = (k * jnp.exp(last - cum)).astype(BF16)
        att = lax.dot_general(qe, ke, nt, preferred_element_type=F32)
        att = jnp.where(tmask, att, 0.0).astype(BF16)
        v = v_ref[pl.ds(r0, GLA_CHUNK), :]
        st = st_ref[...]
        o = (jnp.dot(att, v, preferred_element_type=F32)
             + lax.dot_general(qe, st.astype(BF16), nt, preferred_element_type=F32))
        vt = v.astype(F32).T.astype(BF16)
        st_ref[...] = st * jnp.exp(last) + jnp.dot(vt, kd, preferred_element_type=F32)
        grow = pl.multiple_of(blk * step_rows + c * GLA_CHUNK, GLA_CHUNK)

        @pl.when(fwd)
        def _():
            of_ref[pl.ds(grow, GLA_CHUNK), :] = o

        @pl.when(jnp.logical_not(fwd))
        def _():
            ot = of_ref[pl.ds(grow, GLA_CHUNK), :] + o
            ms = jnp.mean(ot * ot, axis=-1, keepdims=True)
            y = ot * lax.rsqrt(ms + EPS) * gn_ref[...]
            y = y * _silu(r_ref[pl.ds(r0, GLA_CHUNK), :].astype(F32))
            o_ref[pl.ds(r0, GLA_CHUNK), :] = y.astype(o_ref.dtype)

        return carry

    lax.fori_loop(0, cb, chunk_body, 0)


def _gla(proj, wg_pad, b_gate, gn_g, batch, seq, d):
    n = proj.shape[0]
    hk = d // 2 // GLA_HEADS
    hv = d // GLA_HEADS
    cb = min(GLA_CHUNKS_PER_STEP, seq // GLA_CHUNK)
    step_rows = cb * GLA_CHUNK
    nb = seq // step_rows
    a_col = 3 * d // 128

    def rowblk(b, p, i):
        return b * nb + jnp.where(p == 0, i, nb - 1 - i)

    def outblk(b, p, i):
        return b * nb + jnp.where(p == 0, nb - 1, nb - 1 - i)

    return pl.pallas_call(
        functools.partial(_gla_kernel, cb=cb, nb=nb),
        grid=(batch, GLA_HEADS, 2, nb),
        in_specs=[
            pl.BlockSpec((step_rows, hk), lambda b, h, p, i: (rowblk(b, p, i), h)),
            pl.BlockSpec((step_rows, hk), lambda b, h, p, i: (rowblk(b, p, i), GLA_HEADS + h)),
            pl.BlockSpec((step_rows, hv), lambda b, h, p, i: (rowblk(b, p, i), GLA_HEADS + h)),
            pl.BlockSpec((step_rows, hv), lambda b, h, p, i: (rowblk(b, p, i), 2 * GLA_HEADS + h)),
            pl.BlockSpec((step_rows, 128), lambda b, h, p, i: (rowblk(b, p, i), a_col)),
            pl.BlockSpec((None, 128, hk), lambda b, h, p, i: (p, 0, h)),
            pl.BlockSpec((None, 1, hk), lambda b, h, p, i: (p, 0, h)),
            pl.BlockSpec((1, hv), lambda b, h, p, i: (0, h)),
        ],
        out_specs=pl.BlockSpec((step_rows, hv), lambda b, h, p, i: (outblk(b, p, i), h)),
        out_shape=jax.ShapeDtypeStruct((n, d), BF16),
        scratch_shapes=[pltpu.VMEM((hv, hk), F32), pltpu.VMEM((seq, hv), F32)],
        compiler_params=_cparams(("arbitrary", "arbitrary", "arbitrary", "arbitrary")),
        name="gla",
    )(proj, proj, proj, proj, proj, wg_pad, b_gate.reshape(2, 1, -1), gn_g.reshape(1, d))


def _router_kernel(x_ref, g_ref, sc_ref, sh_ref, w_ref, h_ref, eid_ref, rt_ref):
    h = _norm_mod(x_ref[...], g_ref[...], sc_ref[...], sh_ref[...])
    h_ref[...] = h
    logits = jnp.dot(h.astype(BF16), w_ref[...], preferred_element_type=F32)
    lane = lax.broadcasted_iota(jnp.int32, logits.shape, 1)
    gl = jnp.where(lane < MOE_GROUPS, logits, MASK_VALUE)
    gmax = jnp.max(gl, axis=-1, keepdims=True)
    gidx = jnp.min(jnp.where(gl == gmax, lane, ROUTER_LANES), axis=-1, keepdims=True)
    g_w = 1.0 / jnp.sum(jnp.exp(gl - gmax), axis=-1, keepdims=True)
    e_lane = lane - MOE_GROUPS
    in_group = (e_lane >= 0) & (e_lane < MOE_EXPERTS) & ((e_lane >> 3) == gidx)
    el = jnp.where(in_group, logits, MASK_VALUE)
    m1 = jnp.max(el, axis=-1, keepdims=True)
    i1 = jnp.min(jnp.where(el == m1, lane, ROUTER_LANES), axis=-1, keepdims=True)
    el2 = jnp.where(lane == i1, MASK_VALUE, el)
    m2 = jnp.max(el2, axis=-1, keepdims=True)
    i2 = jnp.min(jnp.where(el2 == m2, lane, ROUTER_LANES), axis=-1, keepdims=True)
    t = jnp.exp(m2 - m1)
    w1 = g_w / (1.0 + t)
    w2 = w1 * t
    eid_ref[...] = jnp.where(lane == 0, i1 - MOE_GROUPS, jnp.where(lane == 1, i2 - MOE_GROUPS, 0))
    rt_ref[...] = jnp.where(lane == 0, w1, jnp.where(lane == 1, w2, 0.0))


def _router(x, g, sc, sh, w_router, seq):
    n, d = x.shape
    tm = min(512, seq)
    return pl.pallas_call(
        _router_kernel,
        grid=(n // tm,),
        in_specs=[
            pl.BlockSpec((tm, d), lambda i: (i, 0)),
            pl.BlockSpec((1, d), lambda i: (0, 0)),
            pl.BlockSpec((None, 1, d), lambda i: (i * tm // seq, 0, 0)),
            pl.BlockSpec((None, 1, d), lambda i: (i * tm // seq, 0, 0)),
            pl.BlockSpec((d, ROUTER_LANES), lambda i: (0, 0)),
        ],
        out_specs=[
            pl.BlockSpec((tm, d), lambda i: (i, 0)),
            pl.BlockSpec((tm, ROUTER_LANES), lambda i: (i, 0)),
            pl.BlockSpec((tm, ROUTER_LANES), lambda i: (i, 0)),
        ],
        out_shape=[
            jax.ShapeDtypeStruct((n, d), F32),
            jax.ShapeDtypeStruct((n, ROUTER_LANES), jnp.int32),
            jax.ShapeDtypeStruct((n, ROUTER_LANES), F32),
        ],
        compiler_params=_cparams(("arbitrary",)),
        name="moe_router",
    )(x, g.reshape(1, d), sc, sh, w_router)


def _dispatch_tables(eid2):
    n = eid2.shape[0]
    a = n * 2
    flat_e = eid2.reshape(a)
    onehot = (flat_e[:, None] == jnp.arange(MOE_EXPERTS, dtype=jnp.int32)[None, :]).astype(jnp.int32)
    csum = jnp.cumsum(onehot, axis=0)
    rank = jnp.sum(onehot * csum, axis=1) - 1
    counts = csum[-1]
    padded = (counts + MOE_BLOCK - 1) // MOE_BLOCK * MOE_BLOCK
    pend = jnp.cumsum(padded)
    pstart = pend - padded
    dest = pstart[flat_e] + rank
    n_slots = a + MOE_EXPERTS * MOE_BLOCK
    n_blocks = n_slots // MOE_BLOCK
    slot_tok = jnp.zeros((n_slots,), jnp.int32).at[dest].set(jnp.arange(a, dtype=jnp.int32) // 2)
    bstart = jnp.arange(n_blocks, dtype=jnp.int32) * MOE_BLOCK
    block_expert = jnp.minimum(jnp.searchsorted(pend, bstart, side="right"),
                               MOE_EXPERTS - 1).astype(jnp.int32)
    nvalid = (pend[-1] // MOE_BLOCK).astype(jnp.int32).reshape(1)
    dest2 = dest.reshape(n, 2)
    return slot_tok, block_expert, nvalid, dest2[:, 0], dest2[:, 1]


def _moe_kernel(be_ref, tok_ref, nv_ref, h_hbm, wup_ref, wdn_ref, o_ref,
                xbuf, wup_bf, wdn_bf, sem, *, tb, ff):
    b = pl.program_id(0)
    nvalid = nv_ref[0]

    def gather(blk, slot):
        def body(i, carry):
            tok = tok_ref[blk * tb + i]
            pltpu.make_async_copy(h_hbm.at[pl.ds(tok, 1), :], xbuf.at[slot, pl.ds(i, 1), :],
                                  sem.at[slot]).start()
            return carry
        lax.fori_loop(0, tb, body, 0)

    @pl.when(b == 0)
    def _():
        gather(0, 0)

    @pl.when(b + 1 < nvalid)
    def _():
        gather(b + 1, (b + 1) % 2)

    prev = be_ref[jnp.maximum(b - 1, 0)]

    @pl.when((b < nvalid) & ((b == 0) | (be_ref[b] != prev)))
    def _():
        wup_bf[...] = wup_ref[...].astype(BF16)
        wdn_bf[...] = wdn_ref[...].astype(BF16)

    @pl.when(b < nvalid)
    def _():
        slot = b % 2
        pltpu.make_async_copy(h_hbm.at[pl.ds(0, tb), :], xbuf.at[slot], sem.at[slot]).wait()
        x = xbuf[slot].astype(BF16)
        hcat = jnp.dot(x, wup_bf[...], preferred_element_type=F32)
        act = (_silu(hcat[:, :ff]) * hcat[:, ff:]).astype(BF16)
        o_ref[...] = jnp.dot(act, wdn_bf[...], preferred_element_type=F32)

    @pl.when(b >= nvalid)
    def _():
        o_ref[...] = jnp.zeros_like(o_ref)


def _moe_experts(h2, slot_tok, block_expert, nvalid, w_up, w_down):
    n, d = h2.shape
    n_slots = slot_tok.shape[0]
    tb = MOE_BLOCK
    n_blocks = n_slots // tb
    ff = w_down.shape[1]
    grid_spec = pltpu.PrefetchScalarGridSpec(
        num_scalar_prefetch=3,
        grid=(n_blocks,),
        in_specs=[
            pl.BlockSpec(memory_space=pl.ANY),
            pl.BlockSpec((None, d, 2 * ff), lambda b, be, tok, nv: (be[b], 0, 0)),
            pl.BlockSpec((None, ff, d), lambda b, be, tok, nv: (be[b], 0, 0)),
        ],
        out_specs=pl.BlockSpec((tb, d), lambda b, be, tok, nv: (b, 0)),
        scratch_shapes=[
            pltpu.VMEM((2, tb, d), F32),
            pltpu.VMEM((d, 2 * ff), BF16),
            pltpu.VMEM((ff, d), BF16),
            pltpu.SemaphoreType.DMA((2,)),
        ],
    )
    return pl.pallas_call(
        functools.partial(_moe_kernel, tb=tb, ff=ff),
        grid_spec=grid_spec,
        out_shape=jax.ShapeDtypeStruct((n_slots, d), F32),
        compiler_params=_cparams(("arbitrary",)),
        name="moe_experts",
    )(block_expert, slot_tok, nvalid, h2, w_up, w_down)


def _combine_kernel(d0_ref, d1_ref, yb_hbm, x_ref, rt_ref, gate_ref, o_ref, ybuf, sem, *, tm):
    t = pl.program_id(0)
    nt = pl.num_programs(0)

    def gather(blk, slot):
        def body(i, carry):
            tok = blk * tm + i
            pltpu.make_async_copy(yb_hbm.at[pl.ds(d0_ref[tok], 1), :],
                                  ybuf.at[slot, pl.ds(i, 1), :], sem.at[slot]).start()
            pltpu.make_async_copy(yb_hbm.at[pl.ds(d1_ref[tok], 1), :],
                                  ybuf.at[slot, pl.ds(tm + i, 1), :], sem.at[slot]).start()
            return carry
        lax.fori_loop(0, tm, body, 0)

    @pl.when(t == 0)
    def _():
        gather(0, 0)

    @pl.when(t + 1 < nt)
    def _():
        gather(t + 1, (t + 1) % 2)

    slot = t % 2
    pltpu.make_async_copy(yb_hbm.at[pl.ds(0, 2 * tm), :], ybuf.at[slot], sem.at[slot]).wait()
    rt = rt_ref[...]
    y = rt[:, 0:1] * ybuf[slot, pl.ds(0, tm), :] + rt[:, 1:2] * ybuf[slot, pl.ds(tm, tm), :]
    o_ref[...] = x_ref[...] + gate_ref[...] * y


def _combine(yb, d0, d1, x, route, gate, seq):
    n, d = x.shape
    tm = min(256, seq)
    grid_spec = pltpu.PrefetchScalarGridSpec(
        num_scalar_prefetch=2,
        grid=(n // tm,),
        in_specs=[
            pl.BlockSpec(memory_space=pl.ANY),
            pl.BlockSpec((tm, d), lambda t, d0, d1: (t, 0)),
            pl.BlockSpec((tm, ROUTER_LANES), lambda t, d0, d1: (t, 0)),
            pl.BlockSpec((None, 1, d), lambda t, d0, d1: (t * tm // seq, 0, 0)),
        ],
        out_specs=pl.BlockSpec((tm, d), lambda t, d0, d1: (t, 0)),
        scratch_shapes=[pltpu.VMEM((2, 2 * tm, d), F32), pltpu.SemaphoreType.DMA((2,))],
    )
    return pl.pallas_call(
        functools.partial(_combine_kernel, tm=tm),
        grid_spec=grid_spec,
        out_shape=jax.ShapeDtypeStruct((n, d), F32),
        compiler_params=_cparams(("arbitrary",)),
        name="moe_combine",
    )(d0, d1, yb, x, route, gate)


def _final_norm_kernel(x_ref, g_ref, o_ref):
    x = x_ref[...]
    ms = jnp.mean(x * x, axis=-1, keepdims=True)
    o_ref[...] = x * lax.rsqrt(ms + EPS) * g_ref[...]


def _final_norm(x, g):
    n, d = x.shape
    tm = min(1024, n)
    return pl.pallas_call(
        _final_norm_kernel,
        grid=(n // tm,),
        in_specs=[pl.BlockSpec((tm, d), lambda i: (i, 0)), pl.BlockSpec((1, d), lambda i: (0, 0))],
        out_specs=pl.BlockSpec((tm, d), lambda i: (i, 0)),
        out_shape=jax.ShapeDtypeStruct((n, d), F32),
        compiler_params=_cparams(("arbitrary",)),
        name="final_norm",
    )(x, g.reshape(1, d))


def _proj_tile(nout):
    for tn in (768, 896, 512, 256, 128):
        if nout % tn == 0:
            return tn
    raise ValueError(f"no projection tile for width {nout}")


def kernel(x, c, ada_w, ada_b, norm1_g, norm2_g, na_w_in, na_w_out, na_rpb, gla_w_in, gla_w_gate_up, gla_b_gate, gla_gn_g, gla_w_out, moe_w_router_group, moe_w_router_expert, moe_w_up, moe_w_down, final_g):
    batch, seq, d = x.shape
    depth = ada_w.shape[0]
    n = batch * seq
    assert seq % (GRID_W * NA_ROWS_PER_STEP) == 0 and d % 1024 == 0
    dk = d // 2

    mod = _adaln(c, ada_w, ada_b)
    xf = x.reshape(n, d)
    i_na = 0
    i_gla = 0
    for i in range(depth):
        sh1, sc1, gt1, sh2, sc2, gt2 = [m.reshape(batch, 1, d) for m in jnp.split(mod[i], 6, axis=-1)]
        if i % 2 == 0:
            w_in = na_w_in[i_na].astype(BF16)
            colscale = jnp.concatenate([jnp.full((d,), NA_HEAD_DIM ** -0.5, F32), jnp.ones((2 * d,), F32)])
            qkv = _norm_mod_matmul(xf, norm1_g[i], sc1, sh1, w_in, colscale, seq, _proj_tile(3 * d))
            bias_tab, _ = _na_bias_table(na_rpb[i_na])
            y = _na_attention(qkv, bias_tab, batch, seq, d)
            w_out = na_w_out[i_na].astype(BF16)
            i_na += 1
        else:
            hk = dk // GLA_HEADS
            w_main = gla_w_in[i_gla][:, :3 * d]
            w_a = gla_w_in[i_gla][:, 3 * d:]
            w_in = jnp.concatenate([w_main, w_a, jnp.zeros((d, 128 - 2 * GLA_GATE_RANK), F32)],
                                   axis=1).astype(BF16)
            colscale = jnp.concatenate([jnp.full((dk,), hk ** -0.5, F32),
                                        jnp.ones((3 * d + 128 - dk,), F32)])
            proj = _norm_mod_matmul(xf, norm1_g[i], sc1, sh1, w_in, colscale, seq, _proj_tile(3 * d + 128))
            wg_pad = jnp.zeros((2, 128, dk), F32)
            wg_pad = wg_pad.at[0, :GLA_GATE_RANK].set(gla_w_gate_up[i_gla, 0])
            wg_pad = wg_pad.at[1, GLA_GATE_RANK:2 * GLA_GATE_RANK].set(gla_w_gate_up[i_gla, 1])
            y = _gla(proj, wg_pad.astype(BF16), gla_b_gate[i_gla], gla_gn_g[i_gla], batch, seq, d)
            w_out = gla_w_out[i_gla].astype(BF16)
            i_gla += 1
        xf = _matmul_residual(y, w_out, xf, gt1, seq)

        w_router = jnp.concatenate(
            [moe_w_router_group[i], moe_w_router_expert[i],
             jnp.zeros((d, ROUTER_LANES - MOE_GROUPS - MOE_EXPERTS), F32)], axis=1).astype(BF16)
        h2, eid, route = _router(xf, norm2_g[i], sc2, sh2, w_router, seq)
        slot_tok, block_expert, nvalid, d0, d1 = _dispatch_tables(eid[:, :2])
        yb = _moe_experts(h2, slot_tok, block_expert, nvalid, moe_w_up[i], moe_w_down[i])
        xf = _combine(yb, d0, d1, xf, route, gt2, seq)

    return _final_norm(xf, final_g).reshape(batch, seq, d)
```

```python
import functools

import jax
import jax.numpy as jnp
import numpy as np
from jax import lax
from jax.experimental import pallas as pl
from jax.experimental.pallas import tpu as pltpu

F32 = jnp.float32
BF16 = jnp.bfloat16

EPS = 1e-6
GRID_W = 64
NA_HEAD_DIM = 32
NA_WIN_ROWS = 8
NA_WIN_COLS = 16
NA_HEADS_PER_CHUNK = 4
NA_QBLOCKS = ((0, 24, 0), (24, 16, 16), (40, 24, 32))
NA_KEY_COLS = 32
NA_ROWS_PER_STEP = 8
MASK_VALUE = -1e30

GLA_HEADS = 4
GLA_GATE_RANK = 16
GLA_GATE_NORM = 16.0
GLA_CHUNK = 64
GLA_CHUNKS_PER_STEP = 8

MOE_GROUPS = 4
MOE_EXPERTS_PER_GROUP = 8
MOE_EXPERTS = MOE_GROUPS * MOE_EXPERTS_PER_GROUP
MOE_BLOCK = 256
ROUTER_LANES = 128

VMEM_LIMIT = 56 * 1024 * 1024


def _cparams(sem):
    return pltpu.CompilerParams(dimension_semantics=sem, vmem_limit_bytes=VMEM_LIMIT)


def _silu(v):
    return v * jax.nn.sigmoid(v)


def _adaln_kernel(c_ref, w_ref, b_ref, o_ref):
    cond = _silu(c_ref[...]).astype(BF16)
    w = w_ref[...].astype(BF16)
    o_ref[...] = jnp.dot(cond, w, preferred_element_type=F32) + b_ref[...]


def _adaln(c, ada_w, ada_b):
    depth, d, n6 = ada_w.shape
    b = c.shape[0]
    rows = 16
    cpad = jnp.zeros((rows, d), F32).at[:b].set(c)
    tn = 1024
    out = pl.pallas_call(
        _adaln_kernel,
        grid=(depth, n6 // tn),
        in_specs=[
            pl.BlockSpec((rows, d), lambda i, j: (0, 0)),
            pl.BlockSpec((None, d, tn), lambda i, j: (i, 0, j)),
            pl.BlockSpec((None, 1, tn), lambda i, j: (i, 0, j)),
        ],
        out_specs=pl.BlockSpec((None, rows, tn), lambda i, j: (i, 0, j)),
        out_shape=jax.ShapeDtypeStruct((depth, rows, n6), F32),
        compiler_params=_cparams(("arbitrary", "arbitrary")),
        name="adaln",
    )(cpad, ada_w, ada_b.reshape(depth, 1, n6))
    return out[:, :b]


def _norm_mod(x, g, sc, sh):
    ms = jnp.mean(x * x, axis=-1, keepdims=True)
    y = x * lax.rsqrt(ms + EPS) * g
    return y * (1.0 + sc) + sh


def _nmm_kernel(x_ref, g_ref, sc_ref, sh_ref, w_ref, cs_ref, o_ref, h_ref):
    @pl.when(pl.program_id(1) == 0)
    def _():
        h_ref[...] = _norm_mod(x_ref[...], g_ref[...], sc_ref[...], sh_ref[...]).astype(BF16)

    acc = jnp.dot(h_ref[...], w_ref[...], preferred_element_type=F32)
    o_ref[...] = (acc * cs_ref[...]).astype(o_ref.dtype)


def _norm_mod_matmul(x, g, sc, sh, w, colscale, seq, tn):
    n, d = x.shape
    nout = w.shape[1]
    tm = min(1024, seq)
    return pl.pallas_call(
        _nmm_kernel,
        grid=(n // tm, nout // tn),
        in_specs=[
            pl.BlockSpec((tm, d), lambda i, j: (i, 0)),
            pl.BlockSpec((1, d), lambda i, j: (0, 0)),
            pl.BlockSpec((None, 1, d), lambda i, j: (i * tm // seq, 0, 0)),
            pl.BlockSpec((None, 1, d), lambda i, j: (i * tm // seq, 0, 0)),
            pl.BlockSpec((d, tn), lambda i, j: (0, j)),
            pl.BlockSpec((1, tn), lambda i, j: (0, j)),
        ],
        out_specs=pl.BlockSpec((tm, tn), lambda i, j: (i, j)),
        out_shape=jax.ShapeDtypeStruct((n, nout), BF16),
        scratch_shapes=[pltpu.VMEM((tm, d), BF16)],
        compiler_params=_cparams(("arbitrary", "arbitrary")),
        name="norm_mod_matmul",
    )(x, g.reshape(1, d), sc, sh, w, colscale.reshape(1, nout))


def _mmres_kernel(a_ref, w_ref, res_ref, gate_ref, o_ref):
    acc = jnp.dot(a_ref[...], w_ref[...], preferred_element_type=F32)
    o_ref[...] = res_ref[...] + gate_ref[...] * acc


def _matmul_residual(a, w, res, gate, seq):
    n, k = a.shape
    d = w.shape[1]
    tm = min(512, seq)
    return pl.pallas_call(
        _mmres_kernel,
        grid=(n // tm,),
        in_specs=[
            pl.BlockSpec((tm, k), lambda i: (i, 0)),
            pl.BlockSpec((k, d), lambda i: (0, 0)),
            pl.BlockSpec((tm, d), lambda i: (i, 0)),
            pl.BlockSpec((None, 1, d), lambda i: (i * tm // seq, 0, 0)),
        ],
        out_specs=pl.BlockSpec((tm, d), lambda i: (i, 0)),
        out_shape=jax.ShapeDtypeStruct((n, d), F32),
        compiler_params=_cparams(("arbitrary",)),
        name="matmul_residual",
    )(a, w, res, gate)


def _na_bias_table(rpb):
    heads = rpb.shape[0]
    chunks = heads // NA_HEADS_PER_CHUNK
    wr, wc = NA_WIN_ROWS, NA_WIN_COLS
    rpb = rpb.astype(F32)
    a = jnp.stack([rpb[:, wr - 1 - di:2 * wr - 1 - di, :] for di in range(wr)], axis=1)
    blocks = []
    for qs, nq, ks in NA_QBLOCKS:
        cols = []
        for q in range(qs, qs + nq):
            cstart = min(max(q - wc // 2, 0), GRID_W - wc)
            first = cstart - q + wc - 1
            off = cstart - ks
            cols.append(jnp.pad(a[..., first:first + wc],
                                ((0, 0), (0, 0), (0, 0), (off, NA_KEY_COLS - wc - off)),
                                constant_values=MASK_VALUE))
        t = jnp.stack(cols, axis=3)
        t = t.reshape(chunks, NA_HEADS_PER_CHUNK, wr, wr, nq, NA_KEY_COLS)
        t = t.transpose(0, 2, 1, 4, 3, 5)
        blocks.append(t.reshape(chunks, wr, NA_HEADS_PER_CHUNK * nq, wr * NA_KEY_COLS))
    return jnp.concatenate(blocks, axis=2)


def _na_kernel(q_ref, k_ref, v_ref, bias_ref, o_ref, l_ref, m_ref, p_ref, *, n_rows):
    rblk = pl.program_id(2)
    lane = lax.broadcasted_iota(jnp.int32, (1, 128), 1)
    hmask = [(lane // NA_HEAD_DIM == hp).astype(F32) for hp in range(NA_HEADS_PER_CHUNK)]
    win_tokens = NA_WIN_ROWS * GRID_W
    nkeys = NA_WIN_ROWS * NA_KEY_COLS
    ones = jnp.ones((nkeys, 128), BF16)

    def window(rr):
        r = rblk * NA_ROWS_PER_STEP + rr
        rs = jnp.clip(r - NA_WIN_ROWS // 2, 0, n_rows - NA_WIN_ROWS)
        return pl.multiple_of(rs * GRID_W, GRID_W), r - rs

    def key_block(win, ks):
        return jnp.concatenate([win[i * GRID_W + ks:i * GRID_W + ks + NA_KEY_COLS]
                                for i in range(NA_WIN_ROWS)], axis=0)

    for rr in range(NA_ROWS_PER_STEP):
        w0, di = window(rr)
        qrow = q_ref[pl.ds(rr * GRID_W, GRID_W), :].astype(F32)
        kwin = k_ref[pl.ds(w0, win_tokens), :]
        row0 = 0
        for qs, nq, ks in NA_QBLOCKS:
            nst = NA_HEADS_PER_CHUNK * nq
            qb = qrow[qs:qs + nq]
            qstack = jnp.concatenate([qb * hmask[hp] for hp in range(NA_HEADS_PER_CHUNK)],
                                     axis=0).astype(BF16)
            logits = lax.dot_general(qstack, key_block(kwin, ks), (((1,), (1,)), ((), ())),
                                     preferred_element_type=F32)
            l_ref[rr, pl.ds(row0, nst), :] = logits + bias_ref[di, pl.ds(row0, nst), :]
            row0 += nst

    for rr in range(NA_ROWS_PER_STEP):
        m = jnp.max(l_ref[rr], axis=-1, keepdims=True)
        m_ref[rr] = jnp.broadcast_to(m, m_ref.shape[1:])

    for rr in range(NA_ROWS_PER_STEP):
        m = m_ref[rr]
        p_ref[rr] = jnp.exp(l_ref[rr] - jnp.concatenate([m, m], axis=1)).astype(BF16)

    for rr in range(NA_ROWS_PER_STEP):
        w0, _ = window(rr)
        vwin = v_ref[pl.ds(w0, win_tokens), :]
        outs = []
        row0 = 0
        for qs, nq, ks in NA_QBLOCKS:
            nst = NA_HEADS_PER_CHUNK * nq
            v1 = jnp.concatenate([key_block(vwin, ks), ones], axis=1)
            acc = jnp.dot(p_ref[rr, pl.ds(row0, nst), :], v1, preferred_element_type=F32)
            num = acc[0:nq, :128] * hmask[0]
            den = acc[0:nq, 128:] * hmask[0]
            for hp in range(1, NA_HEADS_PER_CHUNK):
                num = num + acc[hp * nq:(hp + 1) * nq, :128] * hmask[hp]
                den = den + acc[hp * nq:(hp + 1) * nq, 128:] * hmask[hp]
            outs.append(num * (1.0 / den))
            row0 += nst
        orow = jnp.concatenate(outs, axis=0)
        o_ref[pl.ds(rr * GRID_W, GRID_W), :] = orow.astype(o_ref.dtype)


def _na_attention(qkv, bias_tab, batch, seq, d):
    n = qkv.shape[0]
    n_rows = seq // GRID_W
    chunks = d // 128
    step_tokens = NA_ROWS_PER_STEP * GRID_W
    rsteps = n_rows // NA_ROWS_PER_STEP
    nrow = bias_tab.shape[2]
    nkey = bias_tab.shape[3]
    return pl.pallas_call(
        functools.partial(_na_kernel, n_rows=n_rows),
        grid=(batch, chunks, rsteps),
        in_specs=[
            pl.BlockSpec((step_tokens, 128), lambda b, c, r: (b * rsteps + r, c)),
            pl.BlockSpec((seq, 128), lambda b, c, r: (b, chunks + c)),
            pl.BlockSpec((seq, 128), lambda b, c, r: (b, 2 * chunks + c)),
            pl.BlockSpec((None, NA_WIN_ROWS, nrow, nkey), lambda b, c, r: (c, 0, 0, 0)),
        ],
        out_specs=pl.BlockSpec((step_tokens, 128), lambda b, c, r: (b * rsteps + r, c)),
        out_shape=jax.ShapeDtypeStruct((n, d), BF16),
        scratch_shapes=[
            pltpu.VMEM((NA_ROWS_PER_STEP, nrow, nkey), F32),
            pltpu.VMEM((NA_ROWS_PER_STEP, nrow, 128), F32),
            pltpu.VMEM((NA_ROWS_PER_STEP, nrow, nkey), BF16),
        ],
        compiler_params=_cparams(("arbitrary", "arbitrary", "arbitrary")),
        name="na_attention",
    )(qkv, qkv, qkv, bias_tab)


def _log_sigmoid(v):
    return jnp.minimum(v, 0.0) - jnp.log(1.0 + jnp.exp(-jnp.abs(v)))


def _gla_kernel(q_ref, k_ref, v_ref, r_ref, a_ref, wg_ref, bg_ref, gn_ref, o_ref,
                st_ref, of_ref, cum_ref, qe_ref, ke_ref, kdt_ref, att_ref, av_ref, u_ref, dec_ref, os_ref,
                *, cb, nb):
    p = pl.program_id(2)
    i = pl.program_id(3)
    fwd = p == 0
    step_rows = cb * GLA_CHUNK

    @pl.when(i == 0)
    def _():
        st_ref[...] = jnp.zeros_like(st_ref)

    blk = jnp.where(fwd, i, nb - 1 - i)
    tr = lax.broadcasted_iota(jnp.int32, (GLA_CHUNK, GLA_CHUNK), 0)
    tc = lax.broadcasted_iota(jnp.int32, (GLA_CHUNK, GLA_CHUNK), 1)
    tmask = (tr - tc) * (1 - 2 * p) >= 0
    tmat = tmask.astype(BF16)
    nt = (((1,), (1,)), ((), ()))

    gl = jnp.dot(a_ref[...], wg_ref[...], preferred_element_type=F32) + bg_ref[...]
    g = _log_sigmoid(gl) * (1.0 / GLA_GATE_NORM)
    g1 = g.astype(BF16)
    e1 = g - g1.astype(F32)
    g2 = e1.astype(BF16)
    g3 = (e1 - g2.astype(F32)).astype(BF16)
    for c in range(cb):
        rows = slice(c * GLA_CHUNK, (c + 1) * GLA_CHUNK)
        cum_ref[rows, :] = (jnp.dot(tmat, g1[rows], preferred_element_type=F32)
                            + jnp.dot(tmat, g2[rows], preferred_element_type=F32)
                            + jnp.dot(tmat, g3[rows], preferred_element_type=F32))
    for c in range(cb):
        rows = slice(c * GLA_CHUNK, (c + 1) * GLA_CHUNK)
        cum = cum_ref[rows, :]
        last = jnp.where(fwd, cum[GLA_CHUNK - 1:GLA_CHUNK], cum[0:1])
        q = q_ref[rows, :].astype(F32)
        k = k_ref[rows, :].astype(F32)
        qe_ref[c] = (q * jnp.exp(cum)).astype(BF16)
        ke_ref[c] = (k * jnp.exp(-cum)).astype(BF16)
        kdt_ref[c] = (k * jnp.exp(last - cum)).T.astype(BF16)
        dcol = jnp.broadcast_to(jnp.exp(last), (8, last.shape[1])).T
        dec_ref[c] = jnp.broadcast_to(dcol[:, 0:1], dec_ref.shape[1:])
    for c in range(cb):
        att = lax.dot_general(qe_ref[c], ke_ref[c], nt, preferred_element_type=F32)
        att_ref[c] = jnp.where(tmask, att, 0.0).astype(BF16)
    for c in range(cb):
        rows = slice(c * GLA_CHUNK, (c + 1) * GLA_CHUNK)
        v = v_ref[rows, :]
        av_ref[c] = jnp.dot(att_ref[c], v, preferred_element_type=F32)
        u_ref[c] = jnp.dot(kdt_ref[c], v, preferred_element_type=F32)

    lane_tiles = st_ref.shape[1] // 128
    for cc in range(cb):
        c = jnp.where(fwd, cc, cb - 1 - cc)
        st = st_ref[...]
        o = av_ref[c] + jnp.dot(qe_ref[c], st.astype(BF16), preferred_element_type=F32)
        dec = dec_ref[c]
        st_ref[...] = st * jnp.concatenate([dec] * lane_tiles, axis=1) + u_ref[c]
        os_ref[pl.ds(pl.multiple_of(c * GLA_CHUNK, GLA_CHUNK), GLA_CHUNK), :] = o

    g0 = pl.multiple_of(blk * step_rows, step_rows)

    @pl.when(fwd)
    def _():
        of_ref[pl.ds(g0, step_rows), :] = os_ref[...]

    @pl.when(jnp.logical_not(fwd))
    def _():
        ot = of_ref[pl.ds(g0, step_rows), :] + os_ref[...]
        ms = jnp.mean(ot * ot, axis=-1, keepdims=True)
        y = ot * lax.rsqrt(ms + EPS) * gn_ref[...]
        y = y * _silu(r_ref[...].astype(F32))
        o_ref[...] = y.astype(o_ref.dtype)


def _gla(proj, wg_pad, b_gate, gn_g, batch, seq, d):
    n = proj.shape[0]
    hk = d // 2 // GLA_HEADS
    hv = d // GLA_HEADS
    cb = min(GLA_CHUNKS_PER_STEP, seq // GLA_CHUNK)
    step_rows = cb * GLA_CHUNK
    nb = seq // step_rows
    a_col = 3 * d // 128

    def rowblk(b, p, i):
        return b * nb + jnp.where(p == 0, i, nb - 1 - i)

    def outblk(b, p, i):
        return b * nb + jnp.where(p == 0, nb - 1, nb - 1 - i)

    return pl.pallas_call(
        functools.partial(_gla_kernel, cb=cb, nb=nb),
        grid=(batch, GLA_HEADS, 2, nb),
        in_specs=[
            pl.BlockSpec((step_rows, hk), lambda b, h, p, i: (rowblk(b, p, i), h)),
            pl.BlockSpec((step_rows, hk), lambda b, h, p, i: (rowblk(b, p, i), GLA_HEADS + h)),
            pl.BlockSpec((step_rows, hv), lambda b, h, p, i: (rowblk(b, p, i), GLA_HEADS + h)),
            pl.BlockSpec((step_rows, hv), lambda b, h, p, i: (rowblk(b, p, i), 2 * GLA_HEADS + h)),
            pl.BlockSpec((step_rows, 128), lambda b, h, p, i: (rowblk(b, p, i), a_col)),
            pl.BlockSpec((None, 128, hk), lambda b, h, p, i: (p, 0, h)),
            pl.BlockSpec((None, 1, hk), lambda b, h, p, i: (p, 0, h)),
            pl.BlockSpec((1, hv), lambda b, h, p, i: (0, h)),
        ],
        out_specs=pl.BlockSpec((step_rows, hv), lambda b, h, p, i: (outblk(b, p, i), h)),
        out_shape=jax.ShapeDtypeStruct((n, d), BF16),
        scratch_shapes=[
            pltpu.VMEM((hk, hv), F32),
            pltpu.VMEM((seq, hv), F32),
            pltpu.VMEM((step_rows, hk), F32),
            pltpu.VMEM((cb, GLA_CHUNK, hk), BF16),
            pltpu.VMEM((cb, GLA_CHUNK, hk), BF16),
            pltpu.VMEM((cb, hk, GLA_CHUNK), BF16),
            pltpu.VMEM((cb, GLA_CHUNK, GLA_CHUNK), BF16),
            pltpu.VMEM((cb, GLA_CHUNK, hv), F32),
            pltpu.VMEM((cb, hk, hv), F32),
            pltpu.VMEM((cb, hk, 128), F32),
            pltpu.VMEM((step_rows, hv), F32),
        ],
        compiler_params=_cparams(("arbitrary", "arbitrary", "arbitrary", "arbitrary")),
        name="gla",
    )(proj, proj, proj, proj, proj, wg_pad, b_gate.reshape(2, 1, -1), gn_g.reshape(1, d))


def _router_kernel(x_ref, g_ref, sc_ref, sh_ref, w_ref, h_ref, eid_ref, rt_ref):
    h = _norm_mod(x_ref[...], g_ref[...], sc_ref[...], sh_ref[...])
    h_ref[...] = h
    logits = jnp.dot(h.astype(BF16), w_ref[...], preferred_element_type=F32)
    lane = lax.broadcasted_iota(jnp.int32, logits.shape, 1)
    gl = jnp.where(lane < MOE_GROUPS, logits, MASK_VALUE)
    gmax = jnp.max(gl, axis=-1, keepdims=True)
    gidx = jnp.min(jnp.where(gl == gmax, lane, ROUTER_LANES), axis=-1, keepdims=True)
    g_w = 1.0 / jnp.sum(jnp.exp(gl - gmax), axis=-1, keepdims=True)
    e_lane = lane - MOE_GROUPS
    in_group = (e_lane >= 0) & (e_lane < MOE_EXPERTS) & ((e_lane >> 3) == gidx)
    el = jnp.where(in_group, logits, MASK_VALUE)
    m1 = jnp.max(el, axis=-1, keepdims=True)
    i1 = jnp.min(jnp.where(el == m1, lane, ROUTER_LANES), axis=-1, keepdims=True)
    el2 = jnp.where(lane == i1, MASK_VALUE, el)
    m2 = jnp.max(el2, axis=-1, keepdims=True)
    i2 = jnp.min(jnp.where(el2 == m2, lane, ROUTER_LANES), axis=-1, keepdims=True)
    t = jnp.exp(m2 - m1)
    w1 = g_w / (1.0 + t)
    w2 = w1 * t
    eid_ref[...] = jnp.where(lane == 0, i1 - MOE_GROUPS, jnp.where(lane == 1, i2 - MOE_GROUPS, 0))
    rt_ref[...] = jnp.where(lane == 0, w1, jnp.where(lane == 1, w2, 0.0))


def _router(x, g, sc, sh, w_router, seq):
    n, d = x.shape
    tm = min(512, seq)
    return pl.pallas_call(
        _router_kernel,
        grid=(n // tm,),
        in_specs=[
            pl.BlockSpec((tm, d), lambda i: (i, 0)),
            pl.BlockSpec((1, d), lambda i: (0, 0)),
            pl.BlockSpec((None, 1, d), lambda i: (i * tm // seq, 0, 0)),
            pl.BlockSpec((None, 1, d), lambda i: (i * tm // seq, 0, 0)),
            pl.BlockSpec((d, ROUTER_LANES), lambda i: (0, 0)),
        ],
        out_specs=[
            pl.BlockSpec((tm, d), lambda i: (i, 0)),
            pl.BlockSpec((tm, ROUTER_LANES), lambda i: (i, 0)),
            pl.BlockSpec((tm, ROUTER_LANES), lambda i: (i, 0)),
        ],
        out_shape=[
            jax.ShapeDtypeStruct((n, d), F32),
            jax.ShapeDtypeStruct((n, ROUTER_LANES), jnp.int32),
            jax.ShapeDtypeStruct((n, ROUTER_LANES), F32),
        ],
        compiler_params=_cparams(("arbitrary",)),
        name="moe_router",
    )(x, g.reshape(1, d), sc, sh, w_router)


def _dispatch_tables(eid2):
    n = eid2.shape[0]
    a = n * 2
    flat_e = eid2.reshape(a)
    onehot = (flat_e[:, None] == jnp.arange(MOE_EXPERTS, dtype=jnp.int32)[None, :]).astype(jnp.int32)
    csum = jnp.cumsum(onehot, axis=0)
    rank = jnp.sum(onehot * csum, axis=1) - 1
    counts = csum[-1]
    padded = (counts + MOE_BLOCK - 1) // MOE_BLOCK * MOE_BLOCK
    pend = jnp.cumsum(padded)
    pstart = pend - padded
    dest = pstart[flat_e] + rank
    n_slots = a + MOE_EXPERTS * MOE_BLOCK
    n_blocks = n_slots // MOE_BLOCK
    slot_tok = jnp.zeros((n_slots,), jnp.int32).at[dest].set(jnp.arange(a, dtype=jnp.int32) // 2)
    bstart = jnp.arange(n_blocks, dtype=jnp.int32) * MOE_BLOCK
    block_expert = jnp.minimum(jnp.searchsorted(pend, bstart, side="right"),
                               MOE_EXPERTS - 1).astype(jnp.int32)
    nvalid = (pend[-1] // MOE_BLOCK).astype(jnp.int32).reshape(1)
    dest2 = dest.reshape(n, 2)
    return slot_tok, block_expert, nvalid, dest2[:, 0], dest2[:, 1]


def _moe_kernel(be_ref, tok_ref, nv_ref, h_hbm, wup_ref, wdn_ref, o_ref,
                xbuf, wup_bf, wdn_bf, sem, *, tb, ff):
    b = pl.program_id(0)
    nvalid = nv_ref[0]

    def gather(blk, slot):
        def body(i, carry):
            tok = tok_ref[blk * tb + i]
            pltpu.make_async_copy(h_hbm.at[pl.ds(tok, 1), :], xbuf.at[slot, pl.ds(i, 1), :],
                                  sem.at[slot]).start()
            return carry
        lax.fori_loop(0, tb, body, 0)

    @pl.when(b == 0)
    def _():
        gather(0, 0)

    @pl.when(b + 1 < nvalid)
    def _():
        gather(b + 1, (b + 1) % 2)

    prev = be_ref[jnp.maximum(b - 1, 0)]

    @pl.when((b < nvalid) & ((b == 0) | (be_ref[b] != prev)))
    def _():
        wup_bf[...] = wup_ref[...].astype(BF16)
        wdn_bf[...] = wdn_ref[...].astype(BF16)

    @pl.when(b < nvalid)
    def _():
        slot = b % 2
        pltpu.make_async_copy(h_hbm.at[pl.ds(0, tb), :], xbuf.at[slot], sem.at[slot]).wait()
        x = xbuf[slot].astype(BF16)
        hcat = jnp.dot(x, wup_bf[...], preferred_element_type=F32)
        act = (_silu(hcat[:, :ff]) * hcat[:, ff:]).astype(BF16)
        o_ref[...] = jnp.dot(act, wdn_bf[...], preferred_element_type=F32)

    @pl.when(b >= nvalid)
    def _():
        o_ref[...] = jnp.zeros_like(o_ref)


def _moe_experts(h2, slot_tok, block_expert, nvalid, w_up, w_down):
    n, d = h2.shape
    n_slots = slot_tok.shape[0]
    tb = MOE_BLOCK
    n_blocks = n_slots // tb
    ff = w_down.shape[1]
    grid_spec = pltpu.PrefetchScalarGridSpec(
        num_scalar_prefetch=3,
        grid=(n_blocks,),
        in_specs=[
            pl.BlockSpec(memory_space=pl.ANY),
            pl.BlockSpec((None, d, 2 * ff), lambda b, be, tok, nv: (be[b], 0, 0)),
            pl.BlockSpec((None, ff, d), lambda b, be, tok, nv: (be[b], 0, 0)),
        ],
        out_specs=pl.BlockSpec((tb, d), lambda b, be, tok, nv: (b, 0)),
        scratch_shapes=[
            pltpu.VMEM((2, tb, d), F32),
            pltpu.VMEM((d, 2 * ff), BF16),
            pltpu.VMEM((ff, d), BF16),
            pltpu.SemaphoreType.DMA((2,)),
        ],
    )
    return pl.pallas_call(
        functools.partial(_moe_kernel, tb=tb, ff=ff),
        grid_spec=grid_spec,
        out_shape=jax.ShapeDtypeStruct((n_slots, d), F32),
        compiler_params=_cparams(("arbitrary",)),
        name="moe_experts",
    )(block_expert, slot_tok, nvalid, h2, w_up, w_down)


def _combine_kernel(d0_ref, d1_ref, yb_hbm, x_ref, rt_ref, gate_ref, o_ref, ybuf, sem, *, tm):
    t = pl.program_id(0)
    nt = pl.num_programs(0)

    def gather(blk, slot):
        def body(i, carry):
            tok = blk * tm + i
            pltpu.make_async_copy(yb_hbm.at[pl.ds(d0_ref[tok], 1), :],
                                  ybuf.at[slot, pl.ds(i, 1), :], sem.at[slot]).start()
            pltpu.make_async_copy(yb_hbm.at[pl.ds(d1_ref[tok], 1), :],
                                  ybuf.at[slot, pl.ds(tm + i, 1), :], sem.at[slot]).start()
            return carry
        lax.fori_loop(0, tm, body, 0)

    @pl.when(t == 0)
    def _():
        gather(0, 0)

    @pl.when(t + 1 < nt)
    def _():
        gather(t + 1, (t + 1) % 2)

    slot = t % 2
    pltpu.make_async_copy(yb_hbm.at[pl.ds(0, 2 * tm), :], ybuf.at[slot], sem.at[slot]).wait()
    rt = rt_ref[...]
    y = rt[:, 0:1] * ybuf[slot, pl.ds(0, tm), :] + rt[:, 1:2] * ybuf[slot, pl.ds(tm, tm), :]
    o_ref[...] = x_ref[...] + gate_ref[...] * y


def _combine(yb, d0, d1, x, route, gate, seq):
    n, d = x.shape
    tm = min(256, seq)
    grid_spec = pltpu.PrefetchScalarGridSpec(
        num_scalar_prefetch=2,
        grid=(n // tm,),
        in_specs=[
            pl.BlockSpec(memory_space=pl.ANY),
            pl.BlockSpec((tm, d), lambda t, d0, d1: (t, 0)),
            pl.BlockSpec((tm, ROUTER_LANES), lambda t, d0, d1: (t, 0)),
            pl.BlockSpec((None, 1, d), lambda t, d0, d1: (t * tm // seq, 0, 0)),
        ],
        out_specs=pl.BlockSpec((tm, d), lambda t, d0, d1: (t, 0)),
        scratch_shapes=[pltpu.VMEM((2, 2 * tm, d), F32), pltpu.SemaphoreType.DMA((2,))],
    )
    return pl.pallas_call(
        functools.partial(_combine_kernel, tm=tm),
        grid_spec=grid_spec,
        out_shape=jax.ShapeDtypeStruct((n, d), F32),
        compiler_params=_cparams(("arbitrary",)),
        name="moe_combine",
    )(d0, d1, yb, x, route, gate)


def _final_norm_kernel(x_ref, g_ref, o_ref):
    x = x_ref[...]
    ms = jnp.mean(x * x, axis=-1, keepdims=True)
    o_ref[...] = x * lax.rsqrt(ms + EPS) * g_ref[...]


def _final_norm(x, g):
    n, d = x.shape
    tm = min(1024, n)
    return pl.pallas_call(
        _final_norm_kernel,
        grid=(n // tm,),
        in_specs=[pl.BlockSpec((tm, d), lambda i: (i, 0)), pl.BlockSpec((1, d), lambda i: (0, 0))],
        out_specs=pl.BlockSpec((tm, d), lambda i: (i, 0)),
        out_shape=jax.ShapeDtypeStruct((n, d), F32),
        compiler_params=_cparams(("arbitrary",)),
        name="final_norm",
    )(x, g.reshape(1, d))


def _proj_tile(nout):
    for tn in (768, 896, 512, 256, 128):
        if nout % tn == 0:
            return tn
    raise ValueError(f"no projection tile for width {nout}")


def kernel(x, c, ada_w, ada_b, norm1_g, norm2_g, na_w_in, na_w_out, na_rpb, gla_w_in, gla_w_gate_up, gla_b_gate, gla_gn_g, gla_w_out, moe_w_router_group, moe_w_router_expert, moe_w_up, moe_w_down, final_g):
    batch, seq, d = x.shape
    depth = ada_w.shape[0]
    n = batch * seq
    assert seq % (GRID_W * NA_ROWS_PER_STEP) == 0 and d % 1024 == 0
    dk = d // 2

    mod = _adaln(c, ada_w, ada_b)
    xf = x.reshape(n, d)
    i_na = 0
    i_gla = 0
    for i in range(depth):
        sh1, sc1, gt1, sh2, sc2, gt2 = [m.reshape(batch, 1, d) for m in jnp.split(mod[i], 6, axis=-1)]
        if i % 2 == 0:
            w_in = na_w_in[i_na].astype(BF16)
            colscale = jnp.concatenate([jnp.full((d,), NA_HEAD_DIM ** -0.5, F32), jnp.ones((2 * d,), F32)])
            qkv = _norm_mod_matmul(xf, norm1_g[i], sc1, sh1, w_in, colscale, seq, _proj_tile(3 * d))
            bias_tab = _na_bias_table(na_rpb[i_na])
            y = _na_attention(qkv, bias_tab, batch, seq, d)
            w_out = na_w_out[i_na].astype(BF16)
            i_na += 1
        else:
            hk = dk // GLA_HEADS
            w_main = gla_w_in[i_gla][:, :3 * d]
            w_a = gla_w_in[i_gla][:, 3 * d:]
            w_in = jnp.concatenate([w_main, w_a, jnp.zeros((d, 128 - 2 * GLA_GATE_RANK), F32)],
                                   axis=1).astype(BF16)
            colscale = jnp.concatenate([jnp.full((dk,), hk ** -0.5, F32),
                                        jnp.ones((3 * d + 128 - dk,), F32)])
            proj = _norm_mod_matmul(xf, norm1_g[i], sc1, sh1, w_in, colscale, seq, _proj_tile(3 * d + 128))
            wg_pad = jnp.zeros((2, 128, dk), F32)
            wg_pad = wg_pad.at[0, :GLA_GATE_RANK].set(gla_w_gate_up[i_gla, 0])
            wg_pad = wg_pad.at[1, GLA_GATE_RANK:2 * GLA_GATE_RANK].set(gla_w_gate_up[i_gla, 1])
            y = _gla(proj, wg_pad.astype(BF16), gla_b_gate[i_gla], gla_gn_g[i_gla], batch, seq, d)
            w_out = gla_w_out[i_gla].astype(BF16)
            i_gla += 1
        xf = _matmul_residual(y, w_out, xf, gt1, seq)

        w_router = jnp.concatenate(
            [moe_w_router_group[i], moe_w_router_expert[i],
             jnp.zeros((d, ROUTER_LANES - MOE_GROUPS - MOE_EXPERTS), F32)], axis=1).astype(BF16)
        h2, eid, route = _router(xf, norm2_g[i], sc2, sh2, w_router, seq)
        slot_tok, block_expert, nvalid, d0, d1 = _dispatch_tables(eid[:, :2])
        yb = _moe_experts(h2, slot_tok, block_expert, nvalid, moe_w_up[i], moe_w_down[i])
        xf = _combine(yb, d0, d1, xf, route, gt2, seq)

    return _final_norm(xf, final_g).reshape(batch, seq, d)
```

```python
import functools

import jax
import jax.numpy as jnp
import numpy as np
from jax import lax
from jax.experimental import pallas as pl
from jax.experimental.pallas import tpu as pltpu

F32 = jnp.float32
BF16 = jnp.bfloat16

EPS = 1e-6
GRID_W = 64
NA_HEAD_DIM = 32
NA_WIN_ROWS = 8
NA_WIN_COLS = 16
NA_HEADS_PER_CHUNK = 4
NA_QBLOCKS = ((0, 24, 0), (24, 16, 16), (40, 24, 32))
NA_KEY_COLS = 32
NA_ROWS_PER_STEP = 16
MASK_VALUE = -1e30

GLA_HEADS = 4
GLA_GATE_RANK = 16
GLA_GATE_NORM = 16.0
GLA_CHUNK = 64
GLA_CHUNKS_PER_STEP = 8

MOE_GROUPS = 4
MOE_EXPERTS_PER_GROUP = 8
MOE_EXPERTS = MOE_GROUPS * MOE_EXPERTS_PER_GROUP
MOE_BLOCK = 256
ROUTER_LANES = 128

VMEM_LIMIT = 56 * 1024 * 1024


def _cparams(sem):
    return pltpu.CompilerParams(dimension_semantics=sem, vmem_limit_bytes=VMEM_LIMIT)


def _silu(v):
    return v * jax.nn.sigmoid(v)


def _adaln_kernel(c_ref, w_ref, b_ref, o_ref):
    cond = _silu(c_ref[...]).astype(BF16)
    w = w_ref[...].astype(BF16)
    o_ref[...] = jnp.dot(cond, w, preferred_element_type=F32) + b_ref[...]


def _adaln(c, ada_w, ada_b):
    depth, d, n6 = ada_w.shape
    b = c.shape[0]
    rows = 16
    cpad = jnp.zeros((rows, d), F32).at[:b].set(c)
    tn = 1024
    out = pl.pallas_call(
        _adaln_kernel,
        grid=(depth, n6 // tn),
        in_specs=[
            pl.BlockSpec((rows, d), lambda i, j: (0, 0)),
            pl.BlockSpec((None, d, tn), lambda i, j: (i, 0, j)),
            pl.BlockSpec((None, 1, tn), lambda i, j: (i, 0, j)),
        ],
        out_specs=pl.BlockSpec((None, rows, tn), lambda i, j: (i, 0, j)),
        out_shape=jax.ShapeDtypeStruct((depth, rows, n6), F32),
        compiler_params=_cparams(("arbitrary", "arbitrary")),
        name="adaln",
    )(cpad, ada_w, ada_b.reshape(depth, 1, n6))
    return out[:, :b]


def _norm_mod(x, g, sc, sh):
    ms = jnp.mean(x * x, axis=-1, keepdims=True)
    y = x * lax.rsqrt(ms + EPS) * g
    return y * (1.0 + sc) + sh


def _nmm_kernel(x_ref, g_ref, sc_ref, sh_ref, w_ref, cs_ref, o_ref, h_ref):
    @pl.when(pl.program_id(1) == 0)
    def _():
        h_ref[...] = _norm_mod(x_ref[...], g_ref[...], sc_ref[...], sh_ref[...]).astype(BF16)

    acc = jnp.dot(h_ref[...], w_ref[...], preferred_element_type=F32)
    o_ref[...] = (acc * cs_ref[...]).astype(o_ref.dtype)


def _norm_mod_matmul(x, g, sc, sh, w, colscale, seq, tn):
    n, d = x.shape
    nout = w.shape[1]
    tm = min(1024, seq)
    return pl.pallas_call(
        _nmm_kernel,
        grid=(n // tm, nout // tn),
        in_specs=[
            pl.BlockSpec((tm, d), lambda i, j: (i, 0)),
            pl.BlockSpec((1, d), lambda i, j: (0, 0)),
            pl.BlockSpec((None, 1, d), lambda i, j: (i * tm // seq, 0, 0)),
            pl.BlockSpec((None, 1, d), lambda i, j: (i * tm // seq, 0, 0)),
            pl.BlockSpec((d, tn), lambda i, j: (0, j)),
            pl.BlockSpec((1, tn), lambda i, j: (0, j)),
        ],
        out_specs=pl.BlockSpec((tm, tn), lambda i, j: (i, j)),
        out_shape=jax.ShapeDtypeStruct((n, nout), BF16),
        scratch_shapes=[pltpu.VMEM((tm, d), BF16)],
        compiler_params=_cparams(("arbitrary", "arbitrary")),
        name="norm_mod_matmul",
    )(x, g.reshape(1, d), sc, sh, w, colscale.reshape(1, nout))


def _mmres_kernel(a_ref, w_ref, res_ref, gate_ref, o_ref):
    acc = jnp.dot(a_ref[...], w_ref[...], preferred_element_type=F32)
    o_ref[...] = res_ref[...] + gate_ref[...] * acc


def _matmul_residual(a, w, res, gate, seq):
    n, k = a.shape
    d = w.shape[1]
    tm = min(512, seq)
    return pl.pallas_call(
        _mmres_kernel,
        grid=(n // tm,),
        in_specs=[
            pl.BlockSpec((tm, k), lambda i: (i, 0)),
            pl.BlockSpec((k, d), lambda i: (0, 0)),
            pl.BlockSpec((tm, d), lambda i: (i, 0)),
            pl.BlockSpec((None, 1, d), lambda i: (i * tm // seq, 0, 0)),
        ],
        out_specs=pl.BlockSpec((tm, d), lambda i: (i, 0)),
        out_shape=jax.ShapeDtypeStruct((n, d), F32),
        compiler_params=_cparams(("arbitrary",)),
        name="matmul_residual",
    )(a, w, res, gate)


def _na_bias_table(rpb):
    heads = rpb.shape[0]
    chunks = heads // NA_HEADS_PER_CHUNK
    wr, wc = NA_WIN_ROWS, NA_WIN_COLS
    rpb = rpb.astype(F32)
    a = jnp.stack([rpb[:, wr - 1 - di:2 * wr - 1 - di, :] for di in range(wr)], axis=1)
    blocks = []
    for qs, nq, ks in NA_QBLOCKS:
        cols = []
        for q in range(qs, qs + nq):
            cstart = min(max(q - wc // 2, 0), GRID_W - wc)
            first = cstart - q + wc - 1
            off = cstart - ks
            cols.append(jnp.pad(a[..., first:first + wc],
                                ((0, 0), (0, 0), (0, 0), (off, NA_KEY_COLS - wc - off)),
                                constant_values=MASK_VALUE))
        t = jnp.stack(cols, axis=3)
        t = t.reshape(chunks, NA_HEADS_PER_CHUNK, wr, wr, nq, NA_KEY_COLS)
        t = t.transpose(0, 2, 1, 4, 3, 5)
        blocks.append(t.reshape(chunks, wr, NA_HEADS_PER_CHUNK * nq, wr * NA_KEY_COLS))
    return jnp.concatenate(blocks, axis=2)


def _na_kernel(q_ref, k_ref, v_ref, bias_ref, o_ref, l_ref, m_ref, p_ref, *, n_rows):
    rblk = pl.program_id(2)
    lane = lax.broadcasted_iota(jnp.int32, (1, 128), 1)
    hmask = [(lane // NA_HEAD_DIM == hp).astype(F32) for hp in range(NA_HEADS_PER_CHUNK)]
    win_tokens = NA_WIN_ROWS * GRID_W
    nkeys = NA_WIN_ROWS * NA_KEY_COLS
    ones = jnp.ones((nkeys, 128), BF16)

    def window(rr):
        r = rblk * NA_ROWS_PER_STEP + rr
        rs = jnp.clip(r - NA_WIN_ROWS // 2, 0, n_rows - NA_WIN_ROWS)
        return pl.multiple_of(rs * GRID_W, GRID_W), r - rs

    def key_block(win, ks):
        return jnp.concatenate([win[i * GRID_W + ks:i * GRID_W + ks + NA_KEY_COLS]
                                for i in range(NA_WIN_ROWS)], axis=0)

    for rr in range(NA_ROWS_PER_STEP):
        w0, di = window(rr)
        qrow = q_ref[pl.ds(rr * GRID_W, GRID_W), :].astype(F32)
        kwin = k_ref[pl.ds(w0, win_tokens), :]
        row0 = 0
        for qs, nq, ks in NA_QBLOCKS:
            nst = NA_HEADS_PER_CHUNK * nq
            qb = qrow[qs:qs + nq]
            qstack = jnp.concatenate([qb * hmask[hp] for hp in range(NA_HEADS_PER_CHUNK)],
                                     axis=0).astype(BF16)
            logits = lax.dot_general(qstack, key_block(kwin, ks), (((1,), (1,)), ((), ())),
                                     preferred_element_type=F32)
            l_ref[rr, pl.ds(row0, nst), :] = logits + bias_ref[di, pl.ds(row0, nst), :]
            row0 += nst

    for rr in range(NA_ROWS_PER_STEP):
        m = jnp.max(l_ref[rr], axis=-1, keepdims=True)
        m_ref[rr] = jnp.broadcast_to(m, m_ref.shape[1:])

    for rr in range(NA_ROWS_PER_STEP):
        m = m_ref[rr]
        p_ref[rr] = jnp.exp(l_ref[rr] - jnp.concatenate([m, m], axis=1)).astype(BF16)

    for rr in range(NA_ROWS_PER_STEP):
        w0, _ = window(rr)
        vwin = v_ref[pl.ds(w0, win_tokens), :]
        outs = []
        row0 = 0
        for qs, nq, ks in NA_QBLOCKS:
            nst = NA_HEADS_PER_CHUNK * nq
            v1 = jnp.concatenate([key_block(vwin, ks), ones], axis=1)
            acc = jnp.dot(p_ref[rr, pl.ds(row0, nst), :], v1, preferred_element_type=F32)
            num = acc[0:nq, :128] * hmask[0]
            den = acc[0:nq, 128:] * hmask[0]
            for hp in range(1, NA_HEADS_PER_CHUNK):
                num = num + acc[hp * nq:(hp + 1) * nq, :128] * hmask[hp]
                den = den + acc[hp * nq:(hp + 1) * nq, 128:] * hmask[hp]
            outs.append(num * (1.0 / den))
            row0 += nst
        orow = jnp.concatenate(outs, axis=0)
        o_ref[pl.ds(rr * GRID_W, GRID_W), :] = orow.astype(o_ref.dtype)


def _na_attention(qkv, bias_tab, batch, seq, d):
    n = qkv.shape[0]
    n_rows = seq // GRID_W
    chunks = d // 128
    step_tokens = NA_ROWS_PER_STEP * GRID_W
    rsteps = n_rows // NA_ROWS_PER_STEP
    nrow = bias_tab.shape[2]
    nkey = bias_tab.shape[3]
    return pl.pallas_call(
        functools.partial(_na_kernel, n_rows=n_rows),
        grid=(batch, chunks, rsteps),
        in_specs=[
            pl.BlockSpec((step_tokens, 128), lambda b, c, r: (b * rsteps + r, c)),
            pl.BlockSpec((seq, 128), lambda b, c, r: (b, chunks + c)),
            pl.BlockSpec((seq, 128), lambda b, c, r: (b, 2 * chunks + c)),
            pl.BlockSpec((None, NA_WIN_ROWS, nrow, nkey), lambda b, c, r: (c, 0, 0, 0)),
        ],
        out_specs=pl.BlockSpec((step_tokens, 128), lambda b, c, r: (b * rsteps + r, c)),
        out_shape=jax.ShapeDtypeStruct((n, d), BF16),
        scratch_shapes=[
            pltpu.VMEM((NA_ROWS_PER_STEP, nrow, nkey), F32),
            pltpu.VMEM((NA_ROWS_PER_STEP, nrow, 128), F32),
            pltpu.VMEM((NA_ROWS_PER_STEP, nrow, nkey), BF16),
        ],
        compiler_params=_cparams(("arbitrary", "arbitrary", "arbitrary")),
        name="na_attention",
    )(qkv, qkv, qkv, bias_tab)


def _log_sigmoid(v):
    return jnp.minimum(v, 0.0) - jnp.log(1.0 + jnp.exp(-jnp.abs(v)))


def _gla_kernel(q_ref, k_ref, v_ref, r_ref, a_ref, wg_ref, bg_ref, gn_ref, o_ref,
                st_ref, of_ref, cum_ref, qe_ref, ke_ref, kdt_ref, att_ref, av_ref, u_ref, dec_ref, os_ref,
                *, cb, nb):
    p = pl.program_id(2)
    i = pl.program_id(3)
    fwd = p == 0
    step_rows = cb * GLA_CHUNK

    @pl.when(i == 0)
    def _():
        st_ref[...] = jnp.zeros_like(st_ref)

    blk = jnp.where(fwd, i, nb - 1 - i)
    tr = lax.broadcasted_iota(jnp.int32, (GLA_CHUNK, GLA_CHUNK), 0)
    tc = lax.broadcasted_iota(jnp.int32, (GLA_CHUNK, GLA_CHUNK), 1)
    tmask = (tr - tc) * (1 - 2 * p) >= 0
    tmat = tmask.astype(BF16)
    nt = (((1,), (1,)), ((), ()))

    gl = jnp.dot(a_ref[...], wg_ref[...], preferred_element_type=F32) + bg_ref[...]
    g = _log_sigmoid(gl) * (1.0 / GLA_GATE_NORM)
    g1 = g.astype(BF16)
    e1 = g - g1.astype(F32)
    g2 = e1.astype(BF16)
    g3 = (e1 - g2.astype(F32)).astype(BF16)
    for c in range(cb):
        rows = slice(c * GLA_CHUNK, (c + 1) * GLA_CHUNK)
        cum_ref[rows, :] = (jnp.dot(tmat, g1[rows], preferred_element_type=F32)
                            + jnp.dot(tmat, g2[rows], preferred_element_type=F32)
                            + jnp.dot(tmat, g3[rows], preferred_element_type=F32))
    for c in range(cb):
        rows = slice(c * GLA_CHUNK, (c + 1) * GLA_CHUNK)
        cum = cum_ref[rows, :]
        last = jnp.where(fwd, cum[GLA_CHUNK - 1:GLA_CHUNK], cum[0:1])
        q = q_ref[rows, :].astype(F32)
        k = k_ref[rows, :].astype(F32)
        qe_ref[c] = (q * jnp.exp(cum)).astype(BF16)
        ke_ref[c] = (k * jnp.exp(-cum)).astype(BF16)
        kdt_ref[c] = (k * jnp.exp(last - cum)).T.astype(BF16)
        dcol = jnp.broadcast_to(jnp.exp(last), (8, last.shape[1])).T
        dec_ref[c] = jnp.broadcast_to(dcol[:, 0:1], dec_ref.shape[1:])
    for c in range(cb):
        att = lax.dot_general(qe_ref[c], ke_ref[c], nt, preferred_element_type=F32)
        att_ref[c] = jnp.where(tmask, att, 0.0).astype(BF16)
    for c in range(cb):
        rows = slice(c * GLA_CHUNK, (c + 1) * GLA_CHUNK)
        v = v_ref[rows, :]
        av_ref[c] = jnp.dot(att_ref[c], v, preferred_element_type=F32)
        u_ref[c] = jnp.dot(kdt_ref[c], v, preferred_element_type=F32)

    lane_tiles = st_ref.shape[1] // 128
    for cc in range(cb):
        c = jnp.where(fwd, cc, cb - 1 - cc)
        st = st_ref[...]
        o = av_ref[c] + jnp.dot(qe_ref[c], st.astype(BF16), preferred_element_type=F32)
        dec = dec_ref[c]
        st_ref[...] = st * jnp.concatenate([dec] * lane_tiles, axis=1) + u_ref[c]
        os_ref[pl.ds(pl.multiple_of(c * GLA_CHUNK, GLA_CHUNK), GLA_CHUNK), :] = o

    g0 = pl.multiple_of(blk * step_rows, step_rows)

    @pl.when(fwd)
    def _():
        of_ref[pl.ds(g0, step_rows), :] = os_ref[...]

    @pl.when(jnp.logical_not(fwd))
    def _():
        ot = of_ref[pl.ds(g0, step_rows), :] + os_ref[...]
        ms = jnp.mean(ot * ot, axis=-1, keepdims=True)
        y = ot * lax.rsqrt(ms + EPS) * gn_ref[...]
        y = y * _silu(r_ref[...].astype(F32))
        o_ref[...] = y.astype(o_ref.dtype)


def _gla(proj, wg_pad, b_gate, gn_g, batch, seq, d):
    n = proj.shape[0]
    hk = d // 2 // GLA_HEADS
    hv = d // GLA_HEADS
    cb = min(GLA_CHUNKS_PER_STEP, seq // GLA_CHUNK)
    step_rows = cb * GLA_CHUNK
    nb = seq // step_rows
    a_col = 3 * d // 128

    def rowblk(b, p, i):
        return b * nb + jnp.where(p == 0, i, nb - 1 - i)

    def outblk(b, p, i):
        return b * nb + jnp.where(p == 0, nb - 1, nb - 1 - i)

    return pl.pallas_call(
        functools.partial(_gla_kernel, cb=cb, nb=nb),
        grid=(batch, GLA_HEADS, 2, nb),
        in_specs=[
            pl.BlockSpec((step_rows, hk), lambda b, h, p, i: (rowblk(b, p, i), h)),
            pl.BlockSpec((step_rows, hk), lambda b, h, p, i: (rowblk(b, p, i), GLA_HEADS + h)),
            pl.BlockSpec((step_rows, hv), lambda b, h, p, i: (rowblk(b, p, i), GLA_HEADS + h)),
            pl.BlockSpec((step_rows, hv), lambda b, h, p, i: (rowblk(b, p, i), 2 * GLA_HEADS + h)),
            pl.BlockSpec((step_rows, 128), lambda b, h, p, i: (rowblk(b, p, i), a_col)),
            pl.BlockSpec((None, 128, hk), lambda b, h, p, i: (p, 0, h)),
            pl.BlockSpec((None, 1, hk), lambda b, h, p, i: (p, 0, h)),
            pl.BlockSpec((1, hv), lambda b, h, p, i: (0, h)),
        ],
        out_specs=pl.BlockSpec((step_rows, hv), lambda b, h, p, i: (outblk(b, p, i), h)),
        out_shape=jax.ShapeDtypeStruct((n, d), BF16),
        scratch_shapes=[
            pltpu.VMEM((hk, hv), F32),
            pltpu.VMEM((seq, hv), F32),
            pltpu.VMEM((step_rows, hk), F32),
            pltpu.VMEM((cb, GLA_CHUNK, hk), BF16),
            pltpu.VMEM((cb, GLA_CHUNK, hk), BF16),
            pltpu.VMEM((cb, hk, GLA_CHUNK), BF16),
            pltpu.VMEM((cb, GLA_CHUNK, GLA_CHUNK), BF16),
            pltpu.VMEM((cb, GLA_CHUNK, hv), F32),
            pltpu.VMEM((cb, hk, hv), F32),
            pltpu.VMEM((cb, hk, 128), F32),
            pltpu.VMEM((step_rows, hv), F32),
        ],
        compiler_params=_cparams(("arbitrary", "arbitrary", "arbitrary", "arbitrary")),
        name="gla",
    )(proj, proj, proj, proj, proj, wg_pad, b_gate.reshape(2, 1, -1), gn_g.reshape(1, d))


def _router_kernel(x_ref, g_ref, sc_ref, sh_ref, w_ref, h_ref, eid_ref, rt_ref):
    h = _norm_mod(x_ref[...], g_ref[...], sc_ref[...], sh_ref[...])
    h_ref[...] = h
    logits = jnp.dot(h.astype(BF16), w_ref[...], preferred_element_type=F32)
    lane = lax.broadcasted_iota(jnp.int32, logits.shape, 1)
    gl = jnp.where(lane < MOE_GROUPS, logits, MASK_VALUE)
    gmax = jnp.max(gl, axis=-1, keepdims=True)
    gidx = jnp.min(jnp.where(gl == gmax, lane, ROUTER_LANES), axis=-1, keepdims=True)
    g_w = 1.0 / jnp.sum(jnp.exp(gl - gmax), axis=-1, keepdims=True)
    e_lane = lane - MOE_GROUPS
    in_group = (e_lane >= 0) & (e_lane < MOE_EXPERTS) & ((e_lane >> 3) == gidx)
    el = jnp.where(in_group, logits, MASK_VALUE)
    m1 = jnp.max(el, axis=-1, keepdims=True)
    i1 = jnp.min(jnp.where(el == m1, lane, ROUTER_LANES), axis=-1, keepdims=True)
    el2 = jnp.where(lane == i1, MASK_VALUE, el)
    m2 = jnp.max(el2, axis=-1, keepdims=True)
    i2 = jnp.min(jnp.where(el2 == m2, lane, ROUTER_LANES), axis=-1, keepdims=True)
    t = jnp.exp(m2 - m1)
    w1 = g_w / (1.0 + t)
    w2 = w1 * t
    eid_ref[...] = jnp.where(lane == 0, i1 - MOE_GROUPS, jnp.where(lane == 1, i2 - MOE_GROUPS, 0))
    rt_ref[...] = jnp.where(lane == 0, w1, jnp.where(lane == 1, w2, 0.0))


def _router(x, g, sc, sh, w_router, seq):
    n, d = x.shape
    tm = min(512, seq)
    return pl.pallas_call(
        _router_kernel,
        grid=(n // tm,),
        in_specs=[
            pl.BlockSpec((tm, d), lambda i: (i, 0)),
            pl.BlockSpec((1, d), lambda i: (0, 0)),
            pl.BlockSpec((None, 1, d), lambda i: (i * tm // seq, 0, 0)),
            pl.BlockSpec((None, 1, d), lambda i: (i * tm // seq, 0, 0)),
            pl.BlockSpec((d, ROUTER_LANES), lambda i: (0, 0)),
        ],
        out_specs=[
            pl.BlockSpec((tm, d), lambda i: (i, 0)),
            pl.BlockSpec((tm, ROUTER_LANES), lambda i: (i, 0)),
            pl.BlockSpec((tm, ROUTER_LANES), lambda i: (i, 0)),
        ],
        out_shape=[
            jax.ShapeDtypeStruct((n, d), F32),
            jax.ShapeDtypeStruct((n, ROUTER_LANES), jnp.int32),
            jax.ShapeDtypeStruct((n, ROUTER_LANES), F32),
        ],
        compiler_params=_cparams(("arbitrary",)),
        name="moe_router",
    )(x, g.reshape(1, d), sc, sh, w_router)


def _dispatch_tables(eid2):
    n = eid2.shape[0]
    a = n * 2
    flat_e = eid2.reshape(a)
    onehot = (flat_e[:, None] == jnp.arange(MOE_EXPERTS, dtype=jnp.int32)[None, :]).astype(jnp.int32)
    csum = jnp.cumsum(onehot, axis=0)
    rank = jnp.sum(onehot * csum, axis=1) - 1
    counts = csum[-1]
    padded = (counts + MOE_BLOCK - 1) // MOE_BLOCK * MOE_BLOCK
    pend = jnp.cumsum(padded)
    pstart = pend - padded
    dest = pstart[flat_e] + rank
    n_slots = a + MOE_EXPERTS * MOE_BLOCK
    n_blocks = n_slots // MOE_BLOCK
    slot_tok = jnp.zeros((n_slots,), jnp.int32).at[dest].set(jnp.arange(a, dtype=jnp.int32) // 2)
    bstart = jnp.arange(n_blocks, dtype=jnp.int32) * MOE_BLOCK
    block_expert = jnp.minimum(jnp.searchsorted(pend, bstart, side="right"),
                               MOE_EXPERTS - 1).astype(jnp.int32)
    nvalid = (pend[-1] // MOE_BLOCK).astype(jnp.int32).reshape(1)
    dest2 = dest.reshape(n, 2)
    return slot_tok, block_expert, nvalid, dest2[:, 0], dest2[:, 1]


def _moe_kernel(be_ref, tok_ref, nv_ref, h_hbm, wup_ref, wdn_ref, o_ref,
                xbuf, xb_ref, wup_bf, wdn_bf, sem, *, tb, ff):
    b = pl.program_id(0)
    nvalid = nv_ref[0]

    def row_copy(tok, i, slot):
        return pltpu.make_async_copy(h_hbm.at[pl.ds(tok, 1), :], xbuf.at[slot, pl.ds(i, 1), :],
                                     sem.at[slot])

    def wait_rows(slot):
        pltpu.make_async_copy(h_hbm.at[pl.ds(0, tb), :], xbuf.at[slot], sem.at[slot]).wait()

    @pl.when(b == 0)
    def _():
        def body(i, carry):
            row_copy(tok_ref[i], i, 0).start()
            return carry
        lax.fori_loop(0, tb, body, 0)

    prev = be_ref[jnp.maximum(b - 1, 0)]

    @pl.when((b < nvalid) & ((b == 0) | (be_ref[b] != prev)))
    def _():
        wup_bf[...] = wup_ref[...].astype(BF16)
        wdn_bf[...] = wdn_ref[...].astype(BF16)

    @pl.when(b < nvalid)
    def _():
        slot = b % 2
        wait_rows(slot)
        xb_ref[...] = xbuf[slot].astype(BF16)
        base = jnp.minimum(b + 1, nvalid - 1) * tb
        for i in range(tb):
            row_copy(tok_ref[base + i], i, 1 - slot).start()
        hcat = jnp.dot(xb_ref[...], wup_bf[...], preferred_element_type=F32)
        act = (_silu(hcat[:, :ff]) * hcat[:, ff:]).astype(BF16)
        o_ref[...] = jnp.dot(act, wdn_bf[...], preferred_element_type=F32)

        @pl.when(b == nvalid - 1)
        def _():
            wait_rows(1 - slot)

    @pl.when(b >= nvalid)
    def _():
        o_ref[...] = jnp.zeros_like(o_ref)


def _moe_experts(h2, slot_tok, block_expert, nvalid, w_up, w_down, layer):
    n, d = h2.shape
    n_slots = slot_tok.shape[0]
    tb = MOE_BLOCK
    n_blocks = n_slots // tb
    ff = w_down.shape[2]
    grid_spec = pltpu.PrefetchScalarGridSpec(
        num_scalar_prefetch=3,
        grid=(n_blocks,),
        in_specs=[
            pl.BlockSpec(memory_space=pl.ANY),
            pl.BlockSpec((None, None, d, 2 * ff), lambda b, be, tok, nv: (layer, be[b], 0, 0)),
            pl.BlockSpec((None, None, ff, d), lambda b, be, tok, nv: (layer, be[b], 0, 0)),
        ],
        out_specs=pl.BlockSpec((tb, d), lambda b, be, tok, nv: (b, 0)),
        scratch_shapes=[
            pltpu.VMEM((2, tb, d), F32),
            pltpu.VMEM((tb, d), BF16),
            pltpu.VMEM((d, 2 * ff), BF16),
            pltpu.VMEM((ff, d), BF16),
            pltpu.SemaphoreType.DMA((2,)),
        ],
    )
    return pl.pallas_call(
        functools.partial(_moe_kernel, tb=tb, ff=ff),
        grid_spec=grid_spec,
        out_shape=jax.ShapeDtypeStruct((n_slots, d), F32),
        compiler_params=_cparams(("arbitrary",)),
        name="moe_experts",
    )(block_expert, slot_tok, nvalid, h2, w_up, w_down)


def _combine_kernel(d0_ref, d1_ref, yb_hbm, x_ref, rt_ref, gate_ref, o_ref, ybuf, sem, *, tm):
    t = pl.program_id(0)
    nt = pl.num_programs(0)

    def row_copy(src_row, i, slot):
        return pltpu.make_async_copy(yb_hbm.at[pl.ds(src_row, 1), :], ybuf.at[slot, pl.ds(i, 1), :],
                                     sem.at[slot])

    def wait_rows(slot):
        pltpu.make_async_copy(yb_hbm.at[pl.ds(0, 2 * tm), :], ybuf.at[slot], sem.at[slot]).wait()

    def gather(blk, slot):
        base = blk * tm
        for i in range(tm):
            row_copy(d0_ref[base + i], i, slot).start()
            row_copy(d1_ref[base + i], tm + i, slot).start()

    @pl.when(t == 0)
    def _():
        gather(0, 0)

    slot = t % 2
    wait_rows(slot)
    gather(jnp.minimum(t + 1, nt - 1), 1 - slot)
    rt = rt_ref[...]
    y = rt[:, 0:1] * ybuf[slot, pl.ds(0, tm), :] + rt[:, 1:2] * ybuf[slot, pl.ds(tm, tm), :]
    o_ref[...] = x_ref[...] + gate_ref[...] * y

    @pl.when(t == nt - 1)
    def _():
        wait_rows(1 - slot)


def _combine(yb, d0, d1, x, route, gate, seq):
    n, d = x.shape
    tm = min(256, seq)
    grid_spec = pltpu.PrefetchScalarGridSpec(
        num_scalar_prefetch=2,
        grid=(n // tm,),
        in_specs=[
            pl.BlockSpec(memory_space=pl.ANY),
            pl.BlockSpec((tm, d), lambda t, d0, d1: (t, 0)),
            pl.BlockSpec((tm, ROUTER_LANES), lambda t, d0, d1: (t, 0)),
            pl.BlockSpec((None, 1, d), lambda t, d0, d1: (t * tm // seq, 0, 0)),
        ],
        out_specs=pl.BlockSpec((tm, d), lambda t, d0, d1: (t, 0)),
        scratch_shapes=[pltpu.VMEM((2, 2 * tm, d), F32), pltpu.SemaphoreType.DMA((2,))],
    )
    return pl.pallas_call(
        functools.partial(_combine_kernel, tm=tm),
        grid_spec=grid_spec,
        out_shape=jax.ShapeDtypeStruct((n, d), F32),
        compiler_params=_cparams(("arbitrary",)),
        name="moe_combine",
    )(d0, d1, yb, x, route, gate)


def _final_norm_kernel(x_ref, g_ref, o_ref):
    x = x_ref[...]
    ms = jnp.mean(x * x, axis=-1, keepdims=True)
    o_ref[...] = x * lax.rsqrt(ms + EPS) * g_ref[...]


def _final_norm(x, g):
    n, d = x.shape
    tm = min(1024, n)
    return pl.pallas_call(
        _final_norm_kernel,
        grid=(n // tm,),
        in_specs=[pl.BlockSpec((tm, d), lambda i: (i, 0)), pl.BlockSpec((1, d), lambda i: (0, 0))],
        out_specs=pl.BlockSpec((tm, d), lambda i: (i, 0)),
        out_shape=jax.ShapeDtypeStruct((n, d), F32),
        compiler_params=_cparams(("arbitrary",)),
        name="final_norm",
    )(x, g.reshape(1, d))


def _proj_tile(nout):
    for tn in (768, 896, 512, 256, 128):
        if nout % tn == 0:
            return tn
    raise ValueError(f"no projection tile for width {nout}")


def kernel(x, c, ada_w, ada_b, norm1_g, norm2_g, na_w_in, na_w_out, na_rpb, gla_w_in, gla_w_gate_up, gla_b_gate, gla_gn_g, gla_w_out, moe_w_router_group, moe_w_router_expert, moe_w_up, moe_w_down, final_g):
    batch, seq, d = x.shape
    depth = ada_w.shape[0]
    n = batch * seq
    assert seq % (GRID_W * NA_ROWS_PER_STEP) == 0 and d % 1024 == 0
    dk = d // 2

    mod = _adaln(c, ada_w, ada_b)
    xf = x.reshape(n, d)
    i_na = 0
    i_gla = 0
    for i in range(depth):
        sh1, sc1, gt1, sh2, sc2, gt2 = [m.reshape(batch, 1, d) for m in jnp.split(mod[i], 6, axis=-1)]
        if i % 2 == 0:
            w_in = na_w_in[i_na].astype(BF16)
            colscale = jnp.concatenate([jnp.full((d,), NA_HEAD_DIM ** -0.5, F32), jnp.ones((2 * d,), F32)])
            qkv = _norm_mod_matmul(xf, norm1_g[i], sc1, sh1, w_in, colscale, seq, _proj_tile(3 * d))
            bias_tab = _na_bias_table(na_rpb[i_na])
            y = _na_attention(qkv, bias_tab, batch, seq, d)
            w_out = na_w_out[i_na].astype(BF16)
            i_na += 1
        else:
            hk = dk // GLA_HEADS
            w_main = gla_w_in[i_gla][:, :3 * d]
            w_a = gla_w_in[i_gla][:, 3 * d:]
            w_in = jnp.concatenate([w_main, w_a, jnp.zeros((d, 128 - 2 * GLA_GATE_RANK), F32)],
                                   axis=1).astype(BF16)
            colscale = jnp.concatenate([jnp.full((dk,), hk ** -0.5, F32),
                                        jnp.ones((3 * d + 128 - dk,), F32)])
            proj = _norm_mod_matmul(xf, norm1_g[i], sc1, sh1, w_in, colscale, seq, _proj_tile(3 * d + 128))
            wg_pad = jnp.zeros((2, 128, dk), F32)
            wg_pad = wg_pad.at[0, :GLA_GATE_RANK].set(gla_w_gate_up[i_gla, 0])
            wg_pad = wg_pad.at[1, GLA_GATE_RANK:2 * GLA_GATE_RANK].set(gla_w_gate_up[i_gla, 1])
            y = _gla(proj, wg_pad.astype(BF16), gla_b_gate[i_gla], gla_gn_g[i_gla], batch, seq, d)
            w_out = gla_w_out[i_gla].astype(BF16)
            i_gla += 1
        xf = _matmul_residual(y, w_out, xf, gt1, seq)

        w_router = jnp.concatenate(
            [moe_w_router_group[i], moe_w_router_expert[i],
             jnp.zeros((d, ROUTER_LANES - MOE_GROUPS - MOE_EXPERTS), F32)], axis=1).astype(BF16)
        h2, eid, route = _router(xf, norm2_g[i], sc2, sh2, w_router, seq)
        slot_tok, block_expert, nvalid, d0, d1 = _dispatch_tables(eid[:, :2])
        yb = _moe_experts(h2, slot_tok, block_expert, nvalid, moe_w_up, moe_w_down, i)
        xf = _combine(yb, d0, d1, xf, route, gt2, seq)

    return _final_norm(xf, final_g).reshape(batch, seq, d)
```

```python
import functools

import jax
import jax.numpy as jnp
import numpy as np
from jax import lax
from jax.experimental import pallas as pl
from jax.experimental.pallas import tpu as pltpu

F32 = jnp.float32
BF16 = jnp.bfloat16

EPS = 1e-6
GRID_W = 64
NA_HEAD_DIM = 32
NA_WIN_ROWS = 8
NA_WIN_COLS = 16
NA_HEADS_PER_CHUNK = 4
NA_QBLOCKS = ((0, 24, 0), (24, 16, 16), (40, 24, 32))
NA_KEY_COLS = 32
NA_ROWS_PER_STEP = 16
MASK_VALUE = -1e30

GLA_HEADS = 4
GLA_GATE_RANK = 16
GLA_GATE_NORM = 16.0
GLA_CHUNK = 64
GLA_CHUNKS_PER_STEP = 8

MOE_GROUPS = 4
MOE_EXPERTS_PER_GROUP = 8
MOE_EXPERTS = MOE_GROUPS * MOE_EXPERTS_PER_GROUP
MOE_BLOCK = 256
ROUTER_LANES = 128

VMEM_LIMIT = 56 * 1024 * 1024


def _cparams(sem):
    return pltpu.CompilerParams(dimension_semantics=sem, vmem_limit_bytes=VMEM_LIMIT)


def _silu(v):
    return v * jax.nn.sigmoid(v)


def _adaln_kernel(c_ref, w_ref, b_ref, o_ref):
    cond = _silu(c_ref[...]).astype(BF16)
    w = w_ref[...].astype(BF16)
    o_ref[...] = jnp.dot(cond, w, preferred_element_type=F32) + b_ref[...]


def _adaln(c, ada_w, ada_b):
    depth, d, n6 = ada_w.shape
    b = c.shape[0]
    rows = 16
    cpad = jnp.zeros((rows, d), F32).at[:b].set(c)
    tn = 1024
    out = pl.pallas_call(
        _adaln_kernel,
        grid=(depth, n6 // tn),
        in_specs=[
            pl.BlockSpec((rows, d), lambda i, j: (0, 0)),
            pl.BlockSpec((None, d, tn), lambda i, j: (i, 0, j)),
            pl.BlockSpec((None, 1, tn), lambda i, j: (i, 0, j)),
        ],
        out_specs=pl.BlockSpec((None, rows, tn), lambda i, j: (i, 0, j)),
        out_shape=jax.ShapeDtypeStruct((depth, rows, n6), F32),
        compiler_params=_cparams(("arbitrary", "arbitrary")),
        name="adaln",
    )(cpad, ada_w, ada_b.reshape(depth, 1, n6))
    return out[:, :b]


def _norm_mod(x, g, sc, sh):
    ms = jnp.mean(x * x, axis=-1, keepdims=True)
    y = x * lax.rsqrt(ms + EPS) * g
    return y * (1.0 + sc) + sh


def _nmm_kernel(x_ref, g_ref, sc_ref, sh_ref, w_ref, cs_ref, o_ref, h_ref):
    @pl.when(pl.program_id(1) == 0)
    def _():
        h_ref[...] = _norm_mod(x_ref[...], g_ref[...], sc_ref[...], sh_ref[...]).astype(BF16)

    acc = jnp.dot(h_ref[...], w_ref[...], preferred_element_type=F32)
    o_ref[...] = (acc * cs_ref[...]).astype(o_ref.dtype)


def _norm_mod_matmul(x, g, sc, sh, w, colscale, seq, tn):
    n, d = x.shape
    nout = w.shape[1]
    tm = min(1024, seq)
    return pl.pallas_call(
        _nmm_kernel,
        grid=(n // tm, nout // tn),
        in_specs=[
            pl.BlockSpec((tm, d), lambda i, j: (i, 0)),
            pl.BlockSpec((1, d), lambda i, j: (0, 0)),
            pl.BlockSpec((None, 1, d), lambda i, j: (i * tm // seq, 0, 0)),
            pl.BlockSpec((None, 1, d), lambda i, j: (i * tm // seq, 0, 0)),
            pl.BlockSpec((d, tn), lambda i, j: (0, j)),
            pl.BlockSpec((1, tn), lambda i, j: (0, j)),
        ],
        out_specs=pl.BlockSpec((tm, tn), lambda i, j: (i, j)),
        out_shape=jax.ShapeDtypeStruct((n, nout), BF16),
        scratch_shapes=[pltpu.VMEM((tm, d), BF16)],
        compiler_params=_cparams(("arbitrary", "arbitrary")),
        name="norm_mod_matmul",
    )(x, g.reshape(1, d), sc, sh, w, colscale.reshape(1, nout))


def _mmres_kernel(a_ref, w_ref, res_ref, gate_ref, o_ref):
    acc = jnp.dot(a_ref[...], w_ref[...], preferred_element_type=F32)
    o_ref[...] = res_ref[...] + gate_ref[...] * acc


def _matmul_residual(a, w, res, gate, seq):
    n, k = a.shape
    d = w.shape[1]
    tm = min(512, seq)
    return pl.pallas_call(
        _mmres_kernel,
        grid=(n // tm,),
        in_specs=[
            pl.BlockSpec((tm, k), lambda i: (i, 0)),
            pl.BlockSpec((k, d), lambda i: (0, 0)),
            pl.BlockSpec((tm, d), lambda i: (i, 0)),
            pl.BlockSpec((None, 1, d), lambda i: (i * tm // seq, 0, 0)),
        ],
        out_specs=pl.BlockSpec((tm, d), lambda i: (i, 0)),
        out_shape=jax.ShapeDtypeStruct((n, d), F32),
        compiler_params=_cparams(("arbitrary",)),
        name="matmul_residual",
    )(a, w, res, gate)


def _na_bias_table(rpb):
    heads = rpb.shape[0]
    chunks = heads // NA_HEADS_PER_CHUNK
    wr, wc = NA_WIN_ROWS, NA_WIN_COLS
    rpb = rpb.astype(F32)
    a = jnp.stack([rpb[:, wr - 1 - di:2 * wr - 1 - di, :] for di in range(wr)], axis=1)
    blocks = []
    for qs, nq, ks in NA_QBLOCKS:
        cols = []
        for q in range(qs, qs + nq):
            cstart = min(max(q - wc // 2, 0), GRID_W - wc)
            first = cstart - q + wc - 1
            off = cstart - ks
            cols.append(jnp.pad(a[..., first:first + wc],
                                ((0, 0), (0, 0), (0, 0), (off, NA_KEY_COLS - wc - off)),
                                constant_values=MASK_VALUE))
        t = jnp.stack(cols, axis=3)
        t = t.reshape(chunks, NA_HEADS_PER_CHUNK, wr, wr, nq, NA_KEY_COLS)
        t = t.transpose(0, 2, 1, 4, 3, 5)
        blocks.append(t.reshape(chunks, wr, NA_HEADS_PER_CHUNK * nq, wr * NA_KEY_COLS))
    return jnp.concatenate(blocks, axis=2)


def _na_kernel(q_ref, k_ref, v_ref, bias_ref, o_ref, l_ref, m_ref, p_ref, *, n_rows):
    rblk = pl.program_id(2)
    lane = lax.broadcasted_iota(jnp.int32, (1, 128), 1)
    hmask = [(lane // NA_HEAD_DIM == hp).astype(F32) for hp in range(NA_HEADS_PER_CHUNK)]
    win_tokens = NA_WIN_ROWS * GRID_W
    nkeys = NA_WIN_ROWS * NA_KEY_COLS
    ones = jnp.ones((nkeys, 128), BF16)

    def window(rr):
        r = rblk * NA_ROWS_PER_STEP + rr
        rs = jnp.clip(r - NA_WIN_ROWS // 2, 0, n_rows - NA_WIN_ROWS)
        return pl.multiple_of(rs * GRID_W, GRID_W), r - rs

    def key_block(win, ks):
        return jnp.concatenate([win[i * GRID_W + ks:i * GRID_W + ks + NA_KEY_COLS]
                                for i in range(NA_WIN_ROWS)], axis=0)

    for rr in range(NA_ROWS_PER_STEP):
        w0, di = window(rr)
        qrow = q_ref[pl.ds(rr * GRID_W, GRID_W), :].astype(F32)
        kwin = k_ref[pl.ds(w0, win_tokens), :]
        row0 = 0
        for qs, nq, ks in NA_QBLOCKS:
            nst = NA_HEADS_PER_CHUNK * nq
            qb = qrow[qs:qs + nq]
            qstack = jnp.concatenate([qb * hmask[hp] for hp in range(NA_HEADS_PER_CHUNK)],
                                     axis=0).astype(BF16)
            logits = lax.dot_general(qstack, key_block(kwin, ks), (((1,), (1,)), ((), ())),
                                     preferred_element_type=F32)
            l_ref[rr, pl.ds(row0, nst), :] = logits + bias_ref[di, pl.ds(row0, nst), :]
            row0 += nst

    for rr in range(NA_ROWS_PER_STEP):
        m = jnp.max(l_ref[rr], axis=-1, keepdims=True)
        m_ref[rr] = jnp.broadcast_to(m, m_ref.shape[1:])

    for rr in range(NA_ROWS_PER_STEP):
        m = m_ref[rr]
        p_ref[rr] = jnp.exp(l_ref[rr] - jnp.concatenate([m, m], axis=1)).astype(BF16)

    for rr in range(NA_ROWS_PER_STEP):
        w0, _ = window(rr)
        vwin = v_ref[pl.ds(w0, win_tokens), :]
        outs = []
        row0 = 0
        for qs, nq, ks in NA_QBLOCKS:
            nst = NA_HEADS_PER_CHUNK * nq
            v1 = jnp.concatenate([key_block(vwin, ks), ones], axis=1)
            acc = jnp.dot(p_ref[rr, pl.ds(row0, nst), :], v1, preferred_element_type=F32)
            num = acc[0:nq, :128] * hmask[0]
            den = acc[0:nq, 128:] * hmask[0]
            for hp in range(1, NA_HEADS_PER_CHUNK):
                num = num + acc[hp * nq:(hp + 1) * nq, :128] * hmask[hp]
                den = den + acc[hp * nq:(hp + 1) * nq, 128:] * hmask[hp]
            outs.append(num * (1.0 / den))
            row0 += nst
        orow = jnp.concatenate(outs, axis=0)
        o_ref[pl.ds(rr * GRID_W, GRID_W), :] = orow.astype(o_ref.dtype)


def _na_attention(qkv, bias_tab, batch, seq, d):
    n = qkv.shape[0]
    n_rows = seq // GRID_W
    chunks = d // 128
    step_tokens = NA_ROWS_PER_STEP * GRID_W
    rsteps = n_rows // NA_ROWS_PER_STEP
    nrow = bias_tab.shape[2]
    nkey = bias_tab.shape[3]
    return pl.pallas_call(
        functools.partial(_na_kernel, n_rows=n_rows),
        grid=(batch, chunks, rsteps),
        in_specs=[
            pl.BlockSpec((step_tokens, 128), lambda b, c, r: (b * rsteps + r, c)),
            pl.BlockSpec((seq, 128), lambda b, c, r: (b, chunks + c)),
            pl.BlockSpec((seq, 128), lambda b, c, r: (b, 2 * chunks + c)),
            pl.BlockSpec((None, NA_WIN_ROWS, nrow, nkey), lambda b, c, r: (c, 0, 0, 0)),
        ],
        out_specs=pl.BlockSpec((step_tokens, 128), lambda b, c, r: (b * rsteps + r, c)),
        out_shape=jax.ShapeDtypeStruct((n, d), BF16),
        scratch_shapes=[
            pltpu.VMEM((NA_ROWS_PER_STEP, nrow, nkey), F32),
            pltpu.VMEM((NA_ROWS_PER_STEP, nrow, 128), F32),
            pltpu.VMEM((NA_ROWS_PER_STEP, nrow, nkey), BF16),
        ],
        compiler_params=_cparams(("arbitrary", "arbitrary", "arbitrary")),
        name="na_attention",
    )(qkv, qkv, qkv, bias_tab)


def _log_sigmoid(v):
    return jnp.minimum(v, 0.0) - jnp.log(1.0 + jnp.exp(-jnp.abs(v)))


def _gla_kernel(q_ref, k_ref, v_ref, r_ref, a_ref, wg_ref, bg_ref, gn_ref, o_ref,
                st_ref, of_ref, cum_ref, qe_ref, ke_ref, kdt_ref, att_ref, av_ref, u_ref, dec_ref, os_ref,
                *, cb, nb):
    p = pl.program_id(2)
    i = pl.program_id(3)
    fwd = p == 0
    step_rows = cb * GLA_CHUNK

    @pl.when(i == 0)
    def _():
        st_ref[...] = jnp.zeros_like(st_ref)

    blk = jnp.where(fwd, i, nb - 1 - i)
    tr = lax.broadcasted_iota(jnp.int32, (GLA_CHUNK, GLA_CHUNK), 0)
    tc = lax.broadcasted_iota(jnp.int32, (GLA_CHUNK, GLA_CHUNK), 1)
    tmask = (tr - tc) * (1 - 2 * p) >= 0
    tmat = tmask.astype(BF16)
    nt = (((1,), (1,)), ((), ()))

    gl = jnp.dot(a_ref[...], wg_ref[...], preferred_element_type=F32) + bg_ref[...]
    g = _log_sigmoid(gl) * (1.0 / GLA_GATE_NORM)
    g1 = g.astype(BF16)
    e1 = g - g1.astype(F32)
    g2 = e1.astype(BF16)
    g3 = (e1 - g2.astype(F32)).astype(BF16)
    for c in range(cb):
        rows = slice(c * GLA_CHUNK, (c + 1) * GLA_CHUNK)
        cum_ref[rows, :] = (jnp.dot(tmat, g1[rows], preferred_element_type=F32)
                            + jnp.dot(tmat, g2[rows], preferred_element_type=F32)
                            + jnp.dot(tmat, g3[rows], preferred_element_type=F32))
    for c in range(cb):
        rows = slice(c * GLA_CHUNK, (c + 1) * GLA_CHUNK)
        cum = cum_ref[rows, :]
        last = jnp.where(fwd, cum[GLA_CHUNK - 1:GLA_CHUNK], cum[0:1])
        q = q_ref[rows, :].astype(F32)
        k = k_ref[rows, :].astype(F32)
        qe_ref[c] = (q * jnp.exp(cum)).astype(BF16)
        ke_ref[c] = (k * jnp.exp(-cum)).astype(BF16)
        kdt_ref[c] = (k * jnp.exp(last - cum)).T.astype(BF16)
        dcol = jnp.broadcast_to(jnp.exp(last), (8, last.shape[1])).T
        dec_ref[c] = jnp.broadcast_to(dcol[:, 0:1], dec_ref.shape[1:])
    for c in range(cb):
        att = lax.dot_general(qe_ref[c], ke_ref[c], nt, preferred_element_type=F32)
        att_ref[c] = jnp.where(tmask, att, 0.0).astype(BF16)
    for c in range(cb):
        rows = slice(c * GLA_CHUNK, (c + 1) * GLA_CHUNK)
        v = v_ref[rows, :]
        av_ref[c] = jnp.dot(att_ref[c], v, preferred_element_type=F32)
        u_ref[c] = jnp.dot(kdt_ref[c], v, preferred_element_type=F32)

    lane_tiles = st_ref.shape[1] // 128
    for cc in range(cb):
        c = jnp.where(fwd, cc, cb - 1 - cc)
        st = st_ref[...]
        o = av_ref[c] + jnp.dot(qe_ref[c], st.astype(BF16), preferred_element_type=F32)
        dec = dec_ref[c]
        st_ref[...] = st * jnp.concatenate([dec] * lane_tiles, axis=1) + u_ref[c]
        os_ref[pl.ds(pl.multiple_of(c * GLA_CHUNK, GLA_CHUNK), GLA_CHUNK), :] = o

    g0 = pl.multiple_of(blk * step_rows, step_rows)

    @pl.when(fwd)
    def _():
        of_ref[pl.ds(g0, step_rows), :] = os_ref[...]

    @pl.when(jnp.logical_not(fwd))
    def _():
        ot = of_ref[pl.ds(g0, step_rows), :] + os_ref[...]
        ms = jnp.mean(ot * ot, axis=-1, keepdims=True)
        y = ot * lax.rsqrt(ms + EPS) * gn_ref[...]
        y = y * _silu(r_ref[...].astype(F32))
        o_ref[...] = y.astype(o_ref.dtype)


def _gla(proj, wg_pad, b_gate, gn_g, batch, seq, d):
    n = proj.shape[0]
    hk = d // 2 // GLA_HEADS
    hv = d // GLA_HEADS
    cb = min(GLA_CHUNKS_PER_STEP, seq // GLA_CHUNK)
    step_rows = cb * GLA_CHUNK
    nb = seq // step_rows
    a_col = 3 * d // 128

    def rowblk(b, p, i):
        return b * nb + jnp.where(p == 0, i, nb - 1 - i)

    def outblk(b, p, i):
        return b * nb + jnp.where(p == 0, nb - 1, nb - 1 - i)

    return pl.pallas_call(
        functools.partial(_gla_kernel, cb=cb, nb=nb),
        grid=(batch, GLA_HEADS, 2, nb),
        in_specs=[
            pl.BlockSpec((step_rows, hk), lambda b, h, p, i: (rowblk(b, p, i), h)),
            pl.BlockSpec((step_rows, hk), lambda b, h, p, i: (rowblk(b, p, i), GLA_HEADS + h)),
            pl.BlockSpec((step_rows, hv), lambda b, h, p, i: (rowblk(b, p, i), GLA_HEADS + h)),
            pl.BlockSpec((step_rows, hv), lambda b, h, p, i: (rowblk(b, p, i), 2 * GLA_HEADS + h)),
            pl.BlockSpec((step_rows, 128), lambda b, h, p, i: (rowblk(b, p, i), a_col)),
            pl.BlockSpec((None, 128, hk), lambda b, h, p, i: (p, 0, h)),
            pl.BlockSpec((None, 1, hk), lambda b, h, p, i: (p, 0, h)),
            pl.BlockSpec((1, hv), lambda b, h, p, i: (0, h)),
        ],
        out_specs=pl.BlockSpec((step_rows, hv), lambda b, h, p, i: (outblk(b, p, i), h)),
        out_shape=jax.ShapeDtypeStruct((n, d), BF16),
        scratch_shapes=[
            pltpu.VMEM((hk, hv), F32),
            pltpu.VMEM((seq, hv), F32),
            pltpu.VMEM((step_rows, hk), F32),
            pltpu.VMEM((cb, GLA_CHUNK, hk), BF16),
            pltpu.VMEM((cb, GLA_CHUNK, hk), BF16),
            pltpu.VMEM((cb, hk, GLA_CHUNK), BF16),
            pltpu.VMEM((cb, GLA_CHUNK, GLA_CHUNK), BF16),
            pltpu.VMEM((cb, GLA_CHUNK, hv), F32),
            pltpu.VMEM((cb, hk, hv), F32),
            pltpu.VMEM((cb, hk, 128), F32),
            pltpu.VMEM((step_rows, hv), F32),
        ],
        compiler_params=_cparams(("arbitrary", "arbitrary", "arbitrary", "arbitrary")),
        name="gla",
    )(proj, proj, proj, proj, proj, wg_pad, b_gate.reshape(2, 1, -1), gn_g.reshape(1, d))


def _router_kernel(x_ref, g_ref, sc_ref, sh_ref, w_ref, h_ref, eid_ref, rt_ref, cnt_ref, run_ref):
    @pl.when(pl.program_id(0) == 0)
    def _():
        run_ref[...] = jnp.zeros_like(run_ref)

    h = _norm_mod(x_ref[...], g_ref[...], sc_ref[...], sh_ref[...])
    hb = h.astype(BF16)
    half = h.shape[1] // 2
    bits = lax.bitcast_convert_type(hb.astype(F32), jnp.uint32)
    h_ref[...] = (bits[:, :half] >> 16) | (bits[:, half:] & jnp.uint32(0xFFFF0000))
    logits = jnp.dot(hb, w_ref[...], preferred_element_type=F32)
    lane = lax.broadcasted_iota(jnp.int32, logits.shape, 1)
    gl = jnp.where(lane < MOE_GROUPS, logits, MASK_VALUE)
    gmax = jnp.max(gl, axis=-1, keepdims=True)
    gidx = jnp.min(jnp.where(gl == gmax, lane, ROUTER_LANES), axis=-1, keepdims=True)
    g_w = 1.0 / jnp.sum(jnp.exp(gl - gmax), axis=-1, keepdims=True)
    e_lane = lane - MOE_GROUPS
    in_group = (e_lane >= 0) & (e_lane < MOE_EXPERTS) & ((e_lane >> 3) == gidx)
    el = jnp.where(in_group, logits, MASK_VALUE)
    m1 = jnp.max(el, axis=-1, keepdims=True)
    i1 = jnp.min(jnp.where(el == m1, lane, ROUTER_LANES), axis=-1, keepdims=True)
    el2 = jnp.where(lane == i1, MASK_VALUE, el)
    m2 = jnp.max(el2, axis=-1, keepdims=True)
    i2 = jnp.min(jnp.where(el2 == m2, lane, ROUTER_LANES), axis=-1, keepdims=True)
    t = jnp.exp(m2 - m1)
    w1 = g_w / (1.0 + t)
    w2 = w1 * t
    rt_ref[...] = jnp.where(lane == 0, w1, jnp.where(lane == 1, w2, 0.0))
    tm = logits.shape[0]
    oh1 = lane == i1
    oh2 = lane == i2
    both = (oh1 | oh2).astype(BF16)
    tr = lax.broadcasted_iota(jnp.int32, (tm, tm), 0)
    tc = lax.broadcasted_iota(jnp.int32, (tm, tm), 1)
    before = (tc < tr).astype(BF16)
    prior = jnp.dot(before, both, preferred_element_type=F32) + run_ref[0:1, :]
    r1 = jnp.sum(jnp.where(oh1, prior, 0.0), axis=-1, keepdims=True).astype(jnp.int32)
    r2 = jnp.sum(jnp.where(oh2, prior, 0.0), axis=-1, keepdims=True).astype(jnp.int32)
    eid_ref[...] = jnp.where(lane == 0, i1 - MOE_GROUPS,
                             jnp.where(lane == 1, i2 - MOE_GROUPS,
                                       jnp.where(lane == 2, r1, jnp.where(lane == 3, r2, 0))))
    run = run_ref[...] + jnp.sum(both.astype(F32), axis=0, keepdims=True)
    run_ref[...] = run
    cnt_ref[...] = run


def _router(x, g, sc, sh, w_router, seq):
    n, d = x.shape
    tm = min(512, seq)
    return pl.pallas_call(
        _router_kernel,
        grid=(n // tm,),
        in_specs=[
            pl.BlockSpec((tm, d), lambda i: (i, 0)),
            pl.BlockSpec((1, d), lambda i: (0, 0)),
            pl.BlockSpec((None, 1, d), lambda i: (i * tm // seq, 0, 0)),
            pl.BlockSpec((None, 1, d), lambda i: (i * tm // seq, 0, 0)),
            pl.BlockSpec((d, ROUTER_LANES), lambda i: (0, 0)),
        ],
        out_specs=[
            pl.BlockSpec((tm, d // 2), lambda i: (i, 0)),
            pl.BlockSpec((tm, ROUTER_LANES), lambda i: (i, 0)),
            pl.BlockSpec((tm, ROUTER_LANES), lambda i: (i, 0)),
            pl.BlockSpec((8, ROUTER_LANES), lambda i: (0, 0)),
        ],
        out_shape=[
            jax.ShapeDtypeStruct((n, d // 2), jnp.uint32),
            jax.ShapeDtypeStruct((n, ROUTER_LANES), jnp.int32),
            jax.ShapeDtypeStruct((n, ROUTER_LANES), F32),
            jax.ShapeDtypeStruct((8, ROUTER_LANES), F32),
        ],
        scratch_shapes=[pltpu.VMEM((8, ROUTER_LANES), F32)],
        compiler_params=_cparams(("arbitrary",)),
        name="moe_router",
    )(x, g.reshape(1, d), sc, sh, w_router)


def _dispatch_tables(eid, cnt):
    n = eid.shape[0]
    a = n * 2
    flat_e = eid[:, 0:2].reshape(a)
    rank = eid[:, 2:4].reshape(a)
    counts = cnt[0, MOE_GROUPS:MOE_GROUPS + MOE_EXPERTS].astype(jnp.int32)
    padded = (counts + MOE_BLOCK - 1) // MOE_BLOCK * MOE_BLOCK
    pend = jnp.cumsum(padded)
    pstart = pend - padded
    onehot = flat_e[:, None] == jnp.arange(MOE_EXPERTS, dtype=jnp.int32)[None, :]
    dest = jnp.sum(jnp.where(onehot, pstart[None, :], 0), axis=1) + rank
    n_slots = a + MOE_EXPERTS * MOE_BLOCK
    n_blocks = n_slots // MOE_BLOCK
    slot_tok = jnp.zeros((n_slots,), jnp.int32).at[dest].set(jnp.arange(a, dtype=jnp.int32) // 2)
    bstart = jnp.arange(n_blocks, dtype=jnp.int32) * MOE_BLOCK
    block_expert = jnp.minimum(jnp.searchsorted(pend, bstart, side="right"),
                               MOE_EXPERTS - 1).astype(jnp.int32)
    nvalid = (pend[-1] // MOE_BLOCK).astype(jnp.int32).reshape(1)
    dest2 = dest.reshape(n, 2)
    return slot_tok, block_expert, nvalid, dest2[:, 0], dest2[:, 1]


def _moe_kernel(be_ref, tok_ref, nv_ref, h_hbm, wup_ref, wdn_ref, o_ref,
                xbuf, xb_ref, wup_bf, wdn_bf, sem, *, tb, ff):
    b = pl.program_id(0)
    nvalid = nv_ref[0]

    def row_copy(tok, i, slot):
        return pltpu.make_async_copy(h_hbm.at[pl.ds(tok, 1), :], xbuf.at[slot, pl.ds(i, 1), :],
                                     sem.at[slot])

    def wait_rows(slot):
        pltpu.make_async_copy(h_hbm.at[pl.ds(0, tb), :], xbuf.at[slot], sem.at[slot]).wait()

    @pl.when(b == 0)
    def _():
        def body(i, carry):
            row_copy(tok_ref[i], i, 0).start()
            return carry
        lax.fori_loop(0, tb, body, 0)

    prev = be_ref[jnp.maximum(b - 1, 0)]

    @pl.when((b < nvalid) & ((b == 0) | (be_ref[b] != prev)))
    def _():
        wup_bf[...] = wup_ref[...].astype(BF16)
        wdn_bf[...] = wdn_ref[...].astype(BF16)

    @pl.when(b < nvalid)
    def _():
        slot = b % 2
        wait_rows(slot)
        xp = xbuf[slot]
        half = xp.shape[1]
        xb_ref[:, :half] = lax.bitcast_convert_type(xp << 16, F32).astype(BF16)
        xb_ref[:, half:] = lax.bitcast_convert_type(xp & jnp.uint32(0xFFFF0000), F32).astype(BF16)
        base = jnp.minimum(b + 1, nvalid - 1) * tb
        for i in range(tb):
            row_copy(tok_ref[base + i], i, 1 - slot).start()
        hcat = jnp.dot(xb_ref[...], wup_bf[...], preferred_element_type=F32)
        act = (_silu(hcat[:, :ff]) * hcat[:, ff:]).astype(BF16)
        o_ref[...] = jnp.dot(act, wdn_bf[...], preferred_element_type=F32)

        @pl.when(b == nvalid - 1)
        def _():
            wait_rows(1 - slot)

    @pl.when(b >= nvalid)
    def _():
        o_ref[...] = jnp.zeros_like(o_ref)


def _moe_experts(h2, slot_tok, block_expert, nvalid, w_up, w_down, layer):
    n = h2.shape[0]
    d = 2 * h2.shape[1]
    n_slots = slot_tok.shape[0]
    tb = MOE_BLOCK
    n_blocks = n_slots // tb
    ff = w_down.shape[2]
    grid_spec = pltpu.PrefetchScalarGridSpec(
        num_scalar_prefetch=3,
        grid=(n_blocks,),
        in_specs=[
            pl.BlockSpec(memory_space=pl.ANY),
            pl.BlockSpec((None, None, d, 2 * ff), lambda b, be, tok, nv: (layer, be[b], 0, 0)),
            pl.BlockSpec((None, None, ff, d), lambda b, be, tok, nv: (layer, be[b], 0, 0)),
        ],
        out_specs=pl.BlockSpec((tb, d), lambda b, be, tok, nv: (b, 0)),
        scratch_shapes=[
            pltpu.VMEM((2, tb, d // 2), jnp.uint32),
            pltpu.VMEM((tb, d), BF16),
            pltpu.VMEM((d, 2 * ff), BF16),
            pltpu.VMEM((ff, d), BF16),
            pltpu.SemaphoreType.DMA((2,)),
        ],
    )
    return pl.pallas_call(
        functools.partial(_moe_kernel, tb=tb, ff=ff),
        grid_spec=grid_spec,
        out_shape=jax.ShapeDtypeStruct((n_slots, d), F32),
        compiler_params=_cparams(("arbitrary",)),
        name="moe_experts",
    )(block_expert, slot_tok, nvalid, h2, w_up, w_down)


def _combine_kernel(d0_ref, d1_ref, yb_hbm, x_ref, rt_ref, gate_ref, o_ref, ybuf, sem, *, tm):
    t = pl.program_id(0)
    nt = pl.num_programs(0)

    def row_copy(src_row, i, slot):
        return pltpu.make_async_copy(yb_hbm.at[pl.ds(src_row, 1), :], ybuf.at[slot, pl.ds(i, 1), :],
                                     sem.at[slot])

    def wait_rows(slot):
        pltpu.make_async_copy(yb_hbm.at[pl.ds(0, 2 * tm), :], ybuf.at[slot], sem.at[slot]).wait()

    def gather(blk, slot):
        base = blk * tm
        for i in range(tm):
            row_copy(d0_ref[base + i], i, slot).start()
            row_copy(d1_ref[base + i], tm + i, slot).start()

    @pl.when(t == 0)
    def _():
        gather(0, 0)

    slot = t % 2
    wait_rows(slot)
    gather(jnp.minimum(t + 1, nt - 1), 1 - slot)
    rt = rt_ref[...]
    y = rt[:, 0:1] * ybuf[slot, pl.ds(0, tm), :] + rt[:, 1:2] * ybuf[slot, pl.ds(tm, tm), :]
    o_ref[...] = x_ref[...] + gate_ref[...] * y

    @pl.when(t == nt - 1)
    def _():
        wait_rows(1 - slot)


def _combine(yb, d0, d1, x, route, gate, seq):
    n, d = x.shape
    tm = min(256, seq)
    grid_spec = pltpu.PrefetchScalarGridSpec(
        num_scalar_prefetch=2,
        grid=(n // tm,),
        in_specs=[
            pl.BlockSpec(memory_space=pl.ANY),
            pl.BlockSpec((tm, d), lambda t, d0, d1: (t, 0)),
            pl.BlockSpec((tm, ROUTER_LANES), lambda t, d0, d1: (t, 0)),
            pl.BlockSpec((None, 1, d), lambda t, d0, d1: (t * tm // seq, 0, 0)),
        ],
        out_specs=pl.BlockSpec((tm, d), lambda t, d0, d1: (t, 0)),
        scratch_shapes=[pltpu.VMEM((2, 2 * tm, d), F32), pltpu.SemaphoreType.DMA((2,))],
    )
    return pl.pallas_call(
        functools.partial(_combine_kernel, tm=tm),
        grid_spec=grid_spec,
        out_shape=jax.ShapeDtypeStruct((n, d), F32),
        compiler_params=_cparams(("arbitrary",)),
        name="moe_combine",
    )(d0, d1, yb, x, route, gate)


def _final_norm_kernel(x_ref, g_ref, o_ref):
    x = x_ref[...]
    ms = jnp.mean(x * x, axis=-1, keepdims=True)
    o_ref[...] = x * lax.rsqrt(ms + EPS) * g_ref[...]


def _final_norm(x, g):
    n, d = x.shape
    tm = min(1024, n)
    return pl.pallas_call(
        _final_norm_kernel,
        grid=(n // tm,),
        in_specs=[pl.BlockSpec((tm, d), lambda i: (i, 0)), pl.BlockSpec((1, d), lambda i: (0, 0))],
        out_specs=pl.BlockSpec((tm, d), lambda i: (i, 0)),
        out_shape=jax.ShapeDtypeStruct((n, d), F32),
        compiler_params=_cparams(("arbitrary",)),
        name="final_norm",
    )(x, g.reshape(1, d))


def _proj_tile(nout):
    for tn in (1536, 896, 768, 512, 256, 128):
        if nout % tn == 0:
            return tn
    raise ValueError(f"no projection tile for width {nout}")


def kernel(x, c, ada_w, ada_b, norm1_g, norm2_g, na_w_in, na_w_out, na_rpb, gla_w_in, gla_w_gate_up, gla_b_gate, gla_gn_g, gla_w_out, moe_w_router_group, moe_w_router_expert, moe_w_up, moe_w_down, final_g):
    batch, seq, d = x.shape
    depth = ada_w.shape[0]
    n = batch * seq
    assert seq % (GRID_W * NA_ROWS_PER_STEP) == 0 and d % 1024 == 0
    dk = d // 2

    mod = _adaln(c, ada_w, ada_b)
    xf = x.reshape(n, d)
    i_na = 0
    i_gla = 0
    for i in range(depth):
        sh1, sc1, gt1, sh2, sc2, gt2 = [m.reshape(batch, 1, d) for m in jnp.split(mod[i], 6, axis=-1)]
        if i % 2 == 0:
            w_in = na_w_in[i_na].astype(BF16)
            colscale = jnp.concatenate([jnp.full((d,), NA_HEAD_DIM ** -0.5, F32), jnp.ones((2 * d,), F32)])
            qkv = _norm_mod_matmul(xf, norm1_g[i], sc1, sh1, w_in, colscale, seq, _proj_tile(3 * d))
            bias_tab = _na_bias_table(na_rpb[i_na])
            y = _na_attention(qkv, bias_tab, batch, seq, d)
            w_out = na_w_out[i_na].astype(BF16)
            i_na += 1
        else:
            hk = dk // GLA_HEADS
            w_main = gla_w_in[i_gla][:, :3 * d]
            w_a = gla_w_in[i_gla][:, 3 * d:]
            w_in = jnp.concatenate([w_main, w_a, jnp.zeros((d, 128 - 2 * GLA_GATE_RANK), F32)],
                                   axis=1).astype(BF16)
            colscale = jnp.concatenate([jnp.full((dk,), hk ** -0.5, F32),
                                        jnp.ones((3 * d + 128 - dk,), F32)])
            proj = _norm_mod_matmul(xf, norm1_g[i], sc1, sh1, w_in, colscale, seq, _proj_tile(3 * d + 128))
            wg_pad = jnp.zeros((2, 128, dk), F32)
            wg_pad = wg_pad.at[0, :GLA_GATE_RANK].set(gla_w_gate_up[i_gla, 0])
            wg_pad = wg_pad.at[1, GLA_GATE_RANK:2 * GLA_GATE_RANK].set(gla_w_gate_up[i_gla, 1])
            y = _gla(proj, wg_pad.astype(BF16), gla_b_gate[i_gla], gla_gn_g[i_gla], batch, seq, d)
            w_out = gla_w_out[i_gla].astype(BF16)
            i_gla += 1
        xf = _matmul_residual(y, w_out, xf, gt1, seq)

        w_router = jnp.concatenate(
            [moe_w_router_group[i], moe_w_router_expert[i],
             jnp.zeros((d, ROUTER_LANES - MOE_GROUPS - MOE_EXPERTS), F32)], axis=1).astype(BF16)
        h2, eid, route, cnt = _router(xf, norm2_g[i], sc2, sh2, w_router, seq)
        slot_tok, block_expert, nvalid, d0, d1 = _dispatch_tables(eid, cnt)
        yb = _moe_experts(h2, slot_tok, block_expert, nvalid, moe_w_up, moe_w_down, i)
        xf = _combine(yb, d0, d1, xf, route, gt2, seq)

    return _final_norm(xf, final_g).reshape(batch, seq, d)
```

```python
import functools

import jax
import jax.numpy as jnp
import numpy as np
from jax import lax
from jax.experimental import pallas as pl
from jax.experimental.pallas import tpu as pltpu

F32 = jnp.float32
BF16 = jnp.bfloat16

EPS = 1e-6
GRID_W = 64
NA_HEAD_DIM = 32
NA_WIN_ROWS = 8
NA_WIN_COLS = 16
NA_HEADS_PER_CHUNK = 4
NA_QBLOCKS = ((0, 24, 0), (24, 16, 16), (40, 24, 32))
NA_KEY_COLS = 32
NA_ROWS_PER_STEP = 16
MASK_VALUE = -1e30

GLA_HEADS = 4
GLA_GATE_RANK = 16
GLA_GATE_NORM = 16.0
GLA_CHUNK = 64
GLA_CHUNKS_PER_STEP = 8

MOE_GROUPS = 4
MOE_EXPERTS_PER_GROUP = 8
MOE_EXPERTS = MOE_GROUPS * MOE_EXPERTS_PER_GROUP
MOE_BLOCK = 256
MOE_GATHER_DEPTH = 3
MOE_FILL_UNROLL = 8
ROUTER_LANES = 128

VMEM_LIMIT = 56 * 1024 * 1024


def _cparams(sem):
    return pltpu.CompilerParams(dimension_semantics=sem, vmem_limit_bytes=VMEM_LIMIT)


def _silu(v):
    return v * jax.nn.sigmoid(v)


def _adaln_kernel(c_ref, w_ref, b_ref, o_ref):
    cond = _silu(c_ref[...]).astype(BF16)
    w = w_ref[...].astype(BF16)
    o_ref[...] = jnp.dot(cond, w, preferred_element_type=F32) + b_ref[...]


def _adaln(c, ada_w, ada_b):
    depth, d, n6 = ada_w.shape
    b = c.shape[0]
    rows = 16
    cpad = jnp.zeros((rows, d), F32).at[:b].set(c)
    tn = 1024
    out = pl.pallas_call(
        _adaln_kernel,
        grid=(depth, n6 // tn),
        in_specs=[
            pl.BlockSpec((rows, d), lambda i, j: (0, 0)),
            pl.BlockSpec((None, d, tn), lambda i, j: (i, 0, j)),
            pl.BlockSpec((None, 1, tn), lambda i, j: (i, 0, j)),
        ],
        out_specs=pl.BlockSpec((None, rows, tn), lambda i, j: (i, 0, j)),
        out_shape=jax.ShapeDtypeStruct((depth, rows, n6), F32),
        compiler_params=_cparams(("arbitrary", "arbitrary")),
        name="adaln",
    )(cpad, ada_w, ada_b.reshape(depth, 1, n6))
    return out[:, :b]


def _norm_mod(x, g, sc, sh):
    ms = jnp.mean(x * x, axis=-1, keepdims=True)
    y = x * lax.rsqrt(ms + EPS) * g
    return y * (1.0 + sc) + sh


def _nmm_kernel(x_ref, g_ref, sc_ref, sh_ref, w_ref, cs_ref, o_ref, h_ref):
    @pl.when(pl.program_id(1) == 0)
    def _():
        h_ref[...] = _norm_mod(x_ref[...], g_ref[...], sc_ref[...], sh_ref[...]).astype(BF16)

    acc = jnp.dot(h_ref[...], w_ref[...], preferred_element_type=F32)
    o_ref[...] = (acc * cs_ref[...]).astype(o_ref.dtype)


def _norm_mod_matmul(x, g, sc, sh, w, colscale, seq, tn):
    n, d = x.shape
    nout = w.shape[1]
    tm = min(1024, seq)
    return pl.pallas_call(
        _nmm_kernel,
        grid=(n // tm, nout // tn),
        in_specs=[
            pl.BlockSpec((tm, d), lambda i, j: (i, 0)),
            pl.BlockSpec((1, d), lambda i, j: (0, 0)),
            pl.BlockSpec((None, 1, d), lambda i, j: (i * tm // seq, 0, 0)),
            pl.BlockSpec((None, 1, d), lambda i, j: (i * tm // seq, 0, 0)),
            pl.BlockSpec((d, tn), lambda i, j: (0, j)),
            pl.BlockSpec((1, tn), lambda i, j: (0, j)),
        ],
        out_specs=pl.BlockSpec((tm, tn), lambda i, j: (i, j)),
        out_shape=jax.ShapeDtypeStruct((n, nout), BF16),
        scratch_shapes=[pltpu.VMEM((tm, d), BF16)],
        compiler_params=_cparams(("arbitrary", "arbitrary")),
        name="norm_mod_matmul",
    )(x, g.reshape(1, d), sc, sh, w, colscale.reshape(1, nout))


def _mmres_kernel(a_ref, w_ref, res_ref, gate_ref, o_ref):
    acc = jnp.dot(a_ref[...], w_ref[...], preferred_element_type=F32)
    o_ref[...] = res_ref[...] + gate_ref[...] * acc


def _matmul_residual(a, w, res, gate, seq):
    n, k = a.shape
    d = w.shape[1]
    tm = min(512, seq)
    return pl.pallas_call(
        _mmres_kernel,
        grid=(n // tm,),
        in_specs=[
            pl.BlockSpec((tm, k), lambda i: (i, 0)),
            pl.BlockSpec((k, d), lambda i: (0, 0)),
            pl.BlockSpec((tm, d), lambda i: (i, 0)),
            pl.BlockSpec((None, 1, d), lambda i: (i * tm // seq, 0, 0)),
        ],
        out_specs=pl.BlockSpec((tm, d), lambda i: (i, 0)),
        out_shape=jax.ShapeDtypeStruct((n, d), F32),
        compiler_params=_cparams(("arbitrary",)),
        name="matmul_residual",
    )(a, w, res, gate)


def _na_bias_table(rpb):
    heads = rpb.shape[0]
    chunks = heads // NA_HEADS_PER_CHUNK
    wr, wc = NA_WIN_ROWS, NA_WIN_COLS
    rpb = rpb.astype(F32)
    a = jnp.stack([rpb[:, wr - 1 - di:2 * wr - 1 - di, :] for di in range(wr)], axis=1)
    blocks = []
    for qs, nq, ks in NA_QBLOCKS:
        cols = []
        for q in range(qs, qs + nq):
            cstart = min(max(q - wc // 2, 0), GRID_W - wc)
            first = cstart - q + wc - 1
            off = cstart - ks
            cols.append(jnp.pad(a[..., first:first + wc],
                                ((0, 0), (0, 0), (0, 0), (off, NA_KEY_COLS - wc - off)),
                                constant_values=MASK_VALUE))
        t = jnp.stack(cols, axis=3)
        t = t.reshape(chunks, NA_HEADS_PER_CHUNK, wr, wr, nq, NA_KEY_COLS)
        t = t.transpose(0, 2, 1, 4, 3, 5)
        blocks.append(t.reshape(chunks, wr, NA_HEADS_PER_CHUNK * nq, wr * NA_KEY_COLS))
    return jnp.concatenate(blocks, axis=2)


def _na_kernel(q_ref, k_ref, v_ref, bias_ref, o_ref, l_ref, m_ref, p_ref, *, n_rows):
    rblk = pl.program_id(2)
    lane = lax.broadcasted_iota(jnp.int32, (1, 128), 1)
    hmask = [(lane // NA_HEAD_DIM == hp).astype(F32) for hp in range(NA_HEADS_PER_CHUNK)]
    win_tokens = NA_WIN_ROWS * GRID_W
    nkeys = NA_WIN_ROWS * NA_KEY_COLS
    ones = jnp.ones((nkeys, 128), BF16)

    def window(rr):
        r = rblk * NA_ROWS_PER_STEP + rr
        rs = jnp.clip(r - NA_WIN_ROWS // 2, 0, n_rows - NA_WIN_ROWS)
        return pl.multiple_of(rs * GRID_W, GRID_W), r - rs

    def key_block(win, ks):
        return jnp.concatenate([win[i * GRID_W + ks:i * GRID_W + ks + NA_KEY_COLS]
                                for i in range(NA_WIN_ROWS)], axis=0)

    for rr in range(NA_ROWS_PER_STEP):
        w0, di = window(rr)
        qrow = q_ref[pl.ds(rr * GRID_W, GRID_W), :].astype(F32)
        kwin = k_ref[pl.ds(w0, win_tokens), :]
        row0 = 0
        for qs, nq, ks in NA_QBLOCKS:
            nst = NA_HEADS_PER_CHUNK * nq
            qb = qrow[qs:qs + nq]
            qstack = jnp.concatenate([qb * hmask[hp] for hp in range(NA_HEADS_PER_CHUNK)],
                                     axis=0).astype(BF16)
            logits = lax.dot_general(qstack, key_block(kwin, ks), (((1,), (1,)), ((), ())),
                                     preferred_element_type=F32)
            l_ref[rr, pl.ds(row0, nst), :] = logits + bias_ref[di, pl.ds(row0, nst), :]
            row0 += nst

    for rr in range(NA_ROWS_PER_STEP):
        m = jnp.max(l_ref[rr], axis=-1, keepdims=True)
        m_ref[rr] = jnp.broadcast_to(m, m_ref.shape[1:])

    for rr in range(NA_ROWS_PER_STEP):
        m = m_ref[rr]
        p_ref[rr] = jnp.exp(l_ref[rr] - jnp.concatenate([m, m], axis=1)).astype(BF16)

    for rr in range(NA_ROWS_PER_STEP):
        w0, _ = window(rr)
        vwin = v_ref[pl.ds(w0, win_tokens), :]
        outs = []
        row0 = 0
        for qs, nq, ks in NA_QBLOCKS:
            nst = NA_HEADS_PER_CHUNK * nq
            v1 = jnp.concatenate([key_block(vwin, ks), ones], axis=1)
            acc = jnp.dot(p_ref[rr, pl.ds(row0, nst), :], v1, preferred_element_type=F32)
            num = acc[0:nq, :128] * hmask[0]
            den = acc[0:nq, 128:] * hmask[0]
            for hp in range(1, NA_HEADS_PER_CHUNK):
                num = num + acc[hp * nq:(hp + 1) * nq, :128] * hmask[hp]
                den = den + acc[hp * nq:(hp + 1) * nq, 128:] * hmask[hp]
            outs.append(num * (1.0 / den))
            row0 += nst
        orow = jnp.concatenate(outs, axis=0)
        o_ref[pl.ds(rr * GRID_W, GRID_W), :] = orow.astype(o_ref.dtype)


def _na_attention(qkv, bias_tab, batch, seq, d):
    n = qkv.shape[0]
    n_rows = seq // GRID_W
    chunks = d // 128
    step_tokens = NA_ROWS_PER_STEP * GRID_W
    rsteps = n_rows // NA_ROWS_PER_STEP
    nrow = bias_tab.shape[2]
    nkey = bias_tab.shape[3]
    return pl.pallas_call(
        functools.partial(_na_kernel, n_rows=n_rows),
        grid=(batch, chunks, rsteps),
        in_specs=[
            pl.BlockSpec((step_tokens, 128), lambda b, c, r: (b * rsteps + r, c)),
            pl.BlockSpec((seq, 128), lambda b, c, r: (b, chunks + c)),
            pl.BlockSpec((seq, 128), lambda b, c, r: (b, 2 * chunks + c)),
            pl.BlockSpec((None, NA_WIN_ROWS, nrow, nkey), lambda b, c, r: (c, 0, 0, 0)),
        ],
        out_specs=pl.BlockSpec((step_tokens, 128), lambda b, c, r: (b * rsteps + r, c)),
        out_shape=jax.ShapeDtypeStruct((n, d), BF16),
        scratch_shapes=[
            pltpu.VMEM((NA_ROWS_PER_STEP, nrow, nkey), F32),
            pltpu.VMEM((NA_ROWS_PER_STEP, nrow, 128), F32),
            pltpu.VMEM((NA_ROWS_PER_STEP, nrow, nkey), BF16),
        ],
        compiler_params=_cparams(("arbitrary", "arbitrary", "arbitrary")),
        name="na_attention",
    )(qkv, qkv, qkv, bias_tab)


def _log_sigmoid(v):
    return jnp.minimum(v, 0.0) - jnp.log(1.0 + jnp.exp(-jnp.abs(v)))


def _gla_kernel(q_ref, k_ref, v_ref, r_ref, a_ref, wg_ref, bg_ref, gn_ref, o_ref,
                st_ref, of_ref, cum_ref, qe_ref, ke_ref, kdt_ref, att_ref, av_ref, u_ref, dec_ref, os_ref,
                *, cb, nb):
    p = pl.program_id(2)
    i = pl.program_id(3)
    fwd = p == 0
    step_rows = cb * GLA_CHUNK

    @pl.when(i == 0)
    def _():
        st_ref[...] = jnp.zeros_like(st_ref)

    blk = jnp.where(fwd, i, nb - 1 - i)
    tr = lax.broadcasted_iota(jnp.int32, (GLA_CHUNK, GLA_CHUNK), 0)
    tc = lax.broadcasted_iota(jnp.int32, (GLA_CHUNK, GLA_CHUNK), 1)
    tmask = (tr - tc) * (1 - 2 * p) >= 0
    tmat = tmask.astype(BF16)
    nt = (((1,), (1,)), ((), ()))

    gl = jnp.dot(a_ref[...], wg_ref[...], preferred_element_type=F32) + bg_ref[...]
    g = _log_sigmoid(gl) * (1.0 / GLA_GATE_NORM)
    g1 = g.astype(BF16)
    e1 = g - g1.astype(F32)
    g2 = e1.astype(BF16)
    g3 = (e1 - g2.astype(F32)).astype(BF16)
    for c in range(cb):
        rows = slice(c * GLA_CHUNK, (c + 1) * GLA_CHUNK)
        cum_ref[rows, :] = (jnp.dot(tmat, g1[rows], preferred_element_type=F32)
                            + jnp.dot(tmat, g2[rows], preferred_element_type=F32)
                            + jnp.dot(tmat, g3[rows], preferred_element_type=F32))
    for c in range(cb):
        rows = slice(c * GLA_CHUNK, (c + 1) * GLA_CHUNK)
        cum = cum_ref[rows, :]
        last = jnp.where(fwd, cum[GLA_CHUNK - 1:GLA_CHUNK], cum[0:1])
        q = q_ref[rows, :].astype(F32)
        k = k_ref[rows, :].astype(F32)
        qe_ref[c] = (q * jnp.exp(cum)).astype(BF16)
        ke_ref[c] = (k * jnp.exp(-cum)).astype(BF16)
        kdt_ref[c] = (k * jnp.exp(last - cum)).T.astype(BF16)
        dcol = jnp.broadcast_to(jnp.exp(last), (8, last.shape[1])).T
        dec_ref[c] = jnp.broadcast_to(dcol[:, 0:1], dec_ref.shape[1:])
    for c in range(cb):
        att = lax.dot_general(qe_ref[c], ke_ref[c], nt, preferred_element_type=F32)
        att_ref[c] = jnp.where(tmask, att, 0.0).astype(BF16)
    for c in range(cb):
        rows = slice(c * GLA_CHUNK, (c + 1) * GLA_CHUNK)
        v = v_ref[rows, :]
        av_ref[c] = jnp.dot(att_ref[c], v, preferred_element_type=F32)
        u_ref[c] = jnp.dot(kdt_ref[c], v, preferred_element_type=F32)

    lane_tiles = st_ref.shape[1] // 128
    for cc in range(cb):
        c = jnp.where(fwd, cc, cb - 1 - cc)
        st = st_ref[...]
        o = av_ref[c] + jnp.dot(qe_ref[c], st.astype(BF16), preferred_element_type=F32)
        dec = dec_ref[c]
        st_ref[...] = st * jnp.concatenate([dec] * lane_tiles, axis=1) + u_ref[c]
        os_ref[pl.ds(pl.multiple_of(c * GLA_CHUNK, GLA_CHUNK), GLA_CHUNK), :] = o

    g0 = pl.multiple_of(blk * step_rows, step_rows)

    @pl.when(fwd)
    def _():
        of_ref[pl.ds(g0, step_rows), :] = os_ref[...]

    @pl.when(jnp.logical_not(fwd))
    def _():
        ot = of_ref[pl.ds(g0, step_rows), :] + os_ref[...]
        ms = jnp.mean(ot * ot, axis=-1, keepdims=True)
        y = ot * lax.rsqrt(ms + EPS) * gn_ref[...]
        y = y * _silu(r_ref[...].astype(F32))
        o_ref[...] = y.astype(o_ref.dtype)


def _gla(proj, wg_pad, b_gate, gn_g, batch, seq, d):
    n = proj.shape[0]
    hk = d // 2 // GLA_HEADS
    hv = d // GLA_HEADS
    cb = min(GLA_CHUNKS_PER_STEP, seq // GLA_CHUNK)
    step_rows = cb * GLA_CHUNK
    nb = seq // step_rows
    a_col = 3 * d // 128

    def rowblk(b, p, i):
        return b * nb + jnp.where(p == 0, i, nb - 1 - i)

    def outblk(b, p, i):
        return b * nb + jnp.where(p == 0, nb - 1, nb - 1 - i)

    return pl.pallas_call(
        functools.partial(_gla_kernel, cb=cb, nb=nb),
        grid=(batch, GLA_HEADS, 2, nb),
        in_specs=[
            pl.BlockSpec((step_rows, hk), lambda b, h, p, i: (rowblk(b, p, i), h)),
            pl.BlockSpec((step_rows, hk), lambda b, h, p, i: (rowblk(b, p, i), GLA_HEADS + h)),
            pl.BlockSpec((step_rows, hv), lambda b, h, p, i: (rowblk(b, p, i), GLA_HEADS + h)),
            pl.BlockSpec((step_rows, hv), lambda b, h, p, i: (rowblk(b, p, i), 2 * GLA_HEADS + h)),
            pl.BlockSpec((step_rows, 128), lambda b, h, p, i: (rowblk(b, p, i), a_col)),
            pl.BlockSpec((None, 128, hk), lambda b, h, p, i: (p, 0, h)),
            pl.BlockSpec((None, 1, hk), lambda b, h, p, i: (p, 0, h)),
            pl.BlockSpec((1, hv), lambda b, h, p, i: (0, h)),
        ],
        out_specs=pl.BlockSpec((step_rows, hv), lambda b, h, p, i: (outblk(b, p, i), h)),
        out_shape=jax.ShapeDtypeStruct((n, d), BF16),
        scratch_shapes=[
            pltpu.VMEM((hk, hv), F32),
            pltpu.VMEM((seq, hv), F32),
            pltpu.VMEM((step_rows, hk), F32),
            pltpu.VMEM((cb, GLA_CHUNK, hk), BF16),
            pltpu.VMEM((cb, GLA_CHUNK, hk), BF16),
            pltpu.VMEM((cb, hk, GLA_CHUNK), BF16),
            pltpu.VMEM((cb, GLA_CHUNK, GLA_CHUNK), BF16),
            pltpu.VMEM((cb, GLA_CHUNK, hv), F32),
            pltpu.VMEM((cb, hk, hv), F32),
            pltpu.VMEM((cb, hk, 128), F32),
            pltpu.VMEM((step_rows, hv), F32),
        ],
        compiler_params=_cparams(("arbitrary", "arbitrary", "arbitrary", "arbitrary")),
        name="gla",
    )(proj, proj, proj, proj, proj, wg_pad, b_gate.reshape(2, 1, -1), gn_g.reshape(1, d))


def _router_kernel(x_ref, g_ref, sc_ref, sh_ref, w_ref, h_ref, eid_ref, rt_ref, cnt_ref, run_ref):
    @pl.when(pl.program_id(0) == 0)
    def _():
        run_ref[...] = jnp.zeros_like(run_ref)

    h = _norm_mod(x_ref[...], g_ref[...], sc_ref[...], sh_ref[...])
    hb = h.astype(BF16)
    half = h.shape[1] // 2
    bits = lax.bitcast_convert_type(hb.astype(F32), jnp.uint32)
    h_ref[...] = (bits[:, :half] >> 16) | (bits[:, half:] & jnp.uint32(0xFFFF0000))
    logits = jnp.dot(hb, w_ref[...], preferred_element_type=F32)
    lane = lax.broadcasted_iota(jnp.int32, logits.shape, 1)
    gl = jnp.where(lane < MOE_GROUPS, logits, MASK_VALUE)
    gmax = jnp.max(gl, axis=-1, keepdims=True)
    gidx = jnp.min(jnp.where(gl == gmax, lane, ROUTER_LANES), axis=-1, keepdims=True)
    g_w = 1.0 / jnp.sum(jnp.exp(gl - gmax), axis=-1, keepdims=True)
    e_lane = lane - MOE_GROUPS
    in_group = (e_lane >= 0) & (e_lane < MOE_EXPERTS) & ((e_lane >> 3) == gidx)
    el = jnp.where(in_group, logits, MASK_VALUE)
    m1 = jnp.max(el, axis=-1, keepdims=True)
    i1 = jnp.min(jnp.where(el == m1, lane, ROUTER_LANES), axis=-1, keepdims=True)
    el2 = jnp.where(lane == i1, MASK_VALUE, el)
    m2 = jnp.max(el2, axis=-1, keepdims=True)
    i2 = jnp.min(jnp.where(el2 == m2, lane, ROUTER_LANES), axis=-1, keepdims=True)
    t = jnp.exp(m2 - m1)
    w1 = g_w / (1.0 + t)
    w2 = w1 * t
    rt_ref[...] = jnp.where(lane == 0, w1, jnp.where(lane == 1, w2, 0.0))
    tm = logits.shape[0]
    oh1 = lane == i1
    oh2 = lane == i2
    both = (oh1 | oh2).astype(BF16)
    tr = lax.broadcasted_iota(jnp.int32, (tm, tm), 0)
    tc = lax.broadcasted_iota(jnp.int32, (tm, tm), 1)
    before = (tc < tr).astype(BF16)
    prior = jnp.dot(before, both, preferred_element_type=F32) + run_ref[0:1, :]
    r1 = jnp.sum(jnp.where(oh1, prior, 0.0), axis=-1, keepdims=True).astype(jnp.int32)
    r2 = jnp.sum(jnp.where(oh2, prior, 0.0), axis=-1, keepdims=True).astype(jnp.int32)
    eid_ref[...] = jnp.where(lane == 0, i1 - MOE_GROUPS,
                             jnp.where(lane == 1, i2 - MOE_GROUPS,
                                       jnp.where(lane == 2, r1, jnp.where(lane == 3, r2, 0))))
    run = run_ref[...] + jnp.sum(both.astype(F32), axis=0, keepdims=True)
    run_ref[...] = run
    cnt_ref[...] = run


def _router(x, g, sc, sh, w_router, seq):
    n, d = x.shape
    tm = min(512, seq)
    return pl.pallas_call(
        _router_kernel,
        grid=(n // tm,),
        in_specs=[
            pl.BlockSpec((tm, d), lambda i: (i, 0)),
            pl.BlockSpec((1, d), lambda i: (0, 0)),
            pl.BlockSpec((None, 1, d), lambda i: (i * tm // seq, 0, 0)),
            pl.BlockSpec((None, 1, d), lambda i: (i * tm // seq, 0, 0)),
            pl.BlockSpec((d, ROUTER_LANES), lambda i: (0, 0)),
        ],
        out_specs=[
            pl.BlockSpec((tm, d // 2), lambda i: (i, 0)),
            pl.BlockSpec((tm, ROUTER_LANES), lambda i: (i, 0)),
            pl.BlockSpec((tm, ROUTER_LANES), lambda i: (i, 0)),
            pl.BlockSpec((8, ROUTER_LANES), lambda i: (0, 0)),
        ],
        out_shape=[
            jax.ShapeDtypeStruct((n, d // 2), jnp.uint32),
            jax.ShapeDtypeStruct((n, ROUTER_LANES), jnp.int32),
            jax.ShapeDtypeStruct((n, ROUTER_LANES), F32),
            jax.ShapeDtypeStruct((8, ROUTER_LANES), F32),
        ],
        scratch_shapes=[pltpu.VMEM((8, ROUTER_LANES), F32)],
        compiler_params=_cparams(("arbitrary",)),
        name="moe_router",
    )(x, g.reshape(1, d), sc, sh, w_router)


def _dispatch_tables(eid, cnt):
    n = eid.shape[0]
    a = n * 2
    flat_e = eid[:, 0:2].reshape(a)
    rank = eid[:, 2:4].reshape(a)
    counts = cnt[0, MOE_GROUPS:MOE_GROUPS + MOE_EXPERTS].astype(jnp.int32)
    padded = (counts + MOE_BLOCK - 1) // MOE_BLOCK * MOE_BLOCK
    pend = jnp.cumsum(padded)
    pstart = pend - padded
    onehot = flat_e[:, None] == jnp.arange(MOE_EXPERTS, dtype=jnp.int32)[None, :]
    dest = jnp.sum(jnp.where(onehot, pstart[None, :], 0), axis=1) + rank
    n_blocks = (a + MOE_EXPERTS * MOE_BLOCK) // MOE_BLOCK
    bstart = jnp.arange(n_blocks, dtype=jnp.int32) * MOE_BLOCK
    block_expert = jnp.minimum(jnp.sum((pend[None, :] <= bstart[:, None]).astype(jnp.int32), axis=1),
                               MOE_EXPERTS - 1)
    nvalid = (pend[-1] // MOE_BLOCK).astype(jnp.int32).reshape(1)
    dest2 = dest.reshape(n, 2)
    return block_expert, nvalid, dest2[:, 0], dest2[:, 1], pstart + counts, pend


def _moe_kernel(be_ref, d0_ref, d1_ref, nv_ref, plo_ref, phi_ref, h_hbm, wup_ref, wdn_ref, o_ref,
                tok_ref, xbuf, xb_ref, wup_bf, wdn_bf, sem, *, tb, ff):
    b = pl.program_id(0)
    nvalid = nv_ref[0]
    n_tok = d0_ref.shape[0]

    def row_copy(tok, i, slot):
        return pltpu.make_async_copy(h_hbm.at[pl.ds(tok, 1), :], xbuf.at[slot, pl.ds(i, 1), :],
                                     sem.at[slot])

    def wait_rows(slot):
        pltpu.make_async_copy(h_hbm.at[pl.ds(0, tb), :], xbuf.at[slot], sem.at[slot]).wait()

    @pl.when(b == 0)
    def _():
        def fill(i, carry):
            for u in range(MOE_FILL_UNROLL):
                t = i * MOE_FILL_UNROLL + u
                tok_ref[d0_ref[t]] = t
                tok_ref[d1_ref[t]] = t
            return carry
        lax.fori_loop(0, n_tok // MOE_FILL_UNROLL, fill, 0)

        def pad_expert(e, carry):
            def pad(s, c):
                tok_ref[s] = 0
                return c
            lax.fori_loop(plo_ref[e], phi_ref[e], pad, 0)
            return carry
        lax.fori_loop(0, MOE_EXPERTS, pad_expert, 0)

        for ahead in range(MOE_GATHER_DEPTH - 1):
            base = jnp.minimum(ahead, nvalid - 1) * tb

            def body(i, carry, base=base, ahead=ahead):
                row_copy(tok_ref[base + i], i, ahead).start()
                return carry
            lax.fori_loop(0, tb, body, 0)

    prev = be_ref[jnp.maximum(b - 1, 0)]

    @pl.when((b < nvalid) & ((b == 0) | (be_ref[b] != prev)))
    def _():
        wup_bf[...] = wup_ref[...].astype(BF16)
        wdn_bf[...] = wdn_ref[...].astype(BF16)

    @pl.when(b < nvalid)
    def _():
        slot = lax.rem(b, MOE_GATHER_DEPTH)
        wait_rows(slot)
        xp = xbuf[slot]
        half = xp.shape[1]
        xb_ref[:, :half] = lax.bitcast_convert_type(xp << 16, F32).astype(BF16)
        xb_ref[:, half:] = lax.bitcast_convert_type(xp & jnp.uint32(0xFFFF0000), F32).astype(BF16)
        ahead = MOE_GATHER_DEPTH - 1
        nslot = lax.rem(b + ahead, MOE_GATHER_DEPTH)
        base = jnp.minimum(b + ahead, nvalid - 1) * tb
        for i in range(tb):
            row_copy(tok_ref[base + i], i, nslot).start()
        hcat = jnp.dot(xb_ref[...], wup_bf[...], preferred_element_type=F32)
        act = (_silu(hcat[:, :ff]) * hcat[:, ff:]).astype(BF16)
        o_ref[...] = jnp.dot(act, wdn_bf[...], preferred_element_type=F32)

        @pl.when(b == nvalid - 1)
        def _():
            for k in range(1, MOE_GATHER_DEPTH):
                wait_rows(lax.rem(b + k, MOE_GATHER_DEPTH))

    @pl.when(b >= nvalid)
    def _():
        o_ref[...] = jnp.zeros_like(o_ref)


def _moe_experts(h2, block_expert, d0, d1, nvalid, pad_lo, pad_hi, w_up, w_down, layer):
    n = h2.shape[0]
    d = 2 * h2.shape[1]
    tb = MOE_BLOCK
    n_blocks = block_expert.shape[0]
    n_slots = n_blocks * tb
    ff = w_down.shape[2]
    grid_spec = pltpu.PrefetchScalarGridSpec(
        num_scalar_prefetch=6,
        grid=(n_blocks,),
        in_specs=[
            pl.BlockSpec(memory_space=pl.ANY),
            pl.BlockSpec((None, None, d, 2 * ff), lambda b, be, *_: (layer, be[b], 0, 0)),
            pl.BlockSpec((None, None, ff, d), lambda b, be, *_: (layer, be[b], 0, 0)),
        ],
        out_specs=pl.BlockSpec((tb, d), lambda b, be, *_: (b, 0)),
        scratch_shapes=[
            pltpu.SMEM((n_slots,), jnp.int32),
            pltpu.VMEM((MOE_GATHER_DEPTH, tb, d // 2), jnp.uint32),
            pltpu.VMEM((tb, d), BF16),
            pltpu.VMEM((d, 2 * ff), BF16),
            pltpu.VMEM((ff, d), BF16),
            pltpu.SemaphoreType.DMA((MOE_GATHER_DEPTH,)),
        ],
    )
    return pl.pallas_call(
        functools.partial(_moe_kernel, tb=tb, ff=ff),
        grid_spec=grid_spec,
        out_shape=jax.ShapeDtypeStruct((n_slots, d), F32),
        compiler_params=_cparams(("arbitrary",)),
        name="moe_experts",
    )(block_expert, d0, d1, nvalid, pad_lo, pad_hi, h2, w_up, w_down)


def _combine_kernel(d0_ref, d1_ref, yb_hbm, x_ref, rt_ref, gate_ref, o_ref, ybuf, sem, *, tm):
    t = pl.program_id(0)
    nt = pl.num_programs(0)

    def row_copy(src_row, i, slot):
        return pltpu.make_async_copy(yb_hbm.at[pl.ds(src_row, 1), :], ybuf.at[slot, pl.ds(i, 1), :],
                                     sem.at[slot])

    def wait_rows(slot):
        pltpu.make_async_copy(yb_hbm.at[pl.ds(0, 2 * tm), :], ybuf.at[slot], sem.at[slot]).wait()

    def gather(blk, slot):
        base = blk * tm
        for i in range(tm):
            row_copy(d0_ref[base + i], i, slot).start()
            row_copy(d1_ref[base + i], tm + i, slot).start()

    ahead = MOE_GATHER_DEPTH - 1

    @pl.when(t == 0)
    def _():
        for k in range(ahead):
            gather(jnp.minimum(k, nt - 1), k)

    slot = lax.rem(t, MOE_GATHER_DEPTH)
    wait_rows(slot)
    gather(jnp.minimum(t + ahead, nt - 1), lax.rem(t + ahead, MOE_GATHER_DEPTH))
    rt = rt_ref[...]
    y = rt[:, 0:1] * ybuf[slot, pl.ds(0, tm), :] + rt[:, 1:2] * ybuf[slot, pl.ds(tm, tm), :]
    o_ref[...] = x_ref[...] + gate_ref[...] * y

    @pl.when(t == nt - 1)
    def _():
        for k in range(1, MOE_GATHER_DEPTH):
            wait_rows(lax.rem(t + k, MOE_GATHER_DEPTH))


def _combine(yb, d0, d1, x, route, gate, seq):
    n, d = x.shape
    tm = min(256, seq)
    grid_spec = pltpu.PrefetchScalarGridSpec(
        num_scalar_prefetch=2,
        grid=(n // tm,),
        in_specs=[
            pl.BlockSpec(memory_space=pl.ANY),
            pl.BlockSpec((tm, d), lambda t, d0, d1: (t, 0)),
            pl.BlockSpec((tm, ROUTER_LANES), lambda t, d0, d1: (t, 0)),
            pl.BlockSpec((None, 1, d), lambda t, d0, d1: (t * tm // seq, 0, 0)),
        ],
        out_specs=pl.BlockSpec((tm, d), lambda t, d0, d1: (t, 0)),
        scratch_shapes=[pltpu.VMEM((MOE_GATHER_DEPTH, 2 * tm, d), F32),
                        pltpu.SemaphoreType.DMA((MOE_GATHER_DEPTH,))],
    )
    return pl.pallas_call(
        functools.partial(_combine_kernel, tm=tm),
        grid_spec=grid_spec,
        out_shape=jax.ShapeDtypeStruct((n, d), F32),
        compiler_params=_cparams(("arbitrary",)),
        name="moe_combine",
    )(d0, d1, yb, x, route, gate)


def _final_norm_kernel(x_ref, g_ref, o_ref):
    x = x_ref[...]
    ms = jnp.mean(x * x, axis=-1, keepdims=True)
    o_ref[...] = x * lax.rsqrt(ms + EPS) * g_ref[...]


def _final_norm(x, g):
    n, d = x.shape
    tm = min(1024, n)
    return pl.pallas_call(
        _final_norm_kernel,
        grid=(n // tm,),
        in_specs=[pl.BlockSpec((tm, d), lambda i: (i, 0)), pl.BlockSpec((1, d), lambda i: (0, 0))],
        out_specs=pl.BlockSpec((tm, d), lambda i: (i, 0)),
        out_shape=jax.ShapeDtypeStruct((n, d), F32),
        compiler_params=_cparams(("arbitrary",)),
        name="final_norm",
    )(x, g.reshape(1, d))


def _proj_tile(nout):
    for tn in (1536, 896, 768, 512, 256, 128):
        if nout % tn == 0:
            return tn
    raise ValueError(f"no projection tile for width {nout}")


def kernel(x, c, ada_w, ada_b, norm1_g, norm2_g, na_w_in, na_w_out, na_rpb, gla_w_in, gla_w_gate_up, gla_b_gate, gla_gn_g, gla_w_out, moe_w_router_group, moe_w_router_expert, moe_w_up, moe_w_down, final_g):
    batch, seq, d = x.shape
    depth = ada_w.shape[0]
    n = batch * seq
    assert seq % (GRID_W * NA_ROWS_PER_STEP) == 0 and d % 1024 == 0
    dk = d // 2

    mod = _adaln(c, ada_w, ada_b)
    xf = x.reshape(n, d)
    i_na = 0
    i_gla = 0
    for i in range(depth):
        sh1, sc1, gt1, sh2, sc2, gt2 = [m.reshape(batch, 1, d) for m in jnp.split(mod[i], 6, axis=-1)]
        if i % 2 == 0:
            w_in = na_w_in[i_na].astype(BF16)
            colscale = jnp.concatenate([jnp.full((d,), NA_HEAD_DIM ** -0.5, F32), jnp.ones((2 * d,), F32)])
            qkv = _norm_mod_matmul(xf, norm1_g[i], sc1, sh1, w_in, colscale, seq, _proj_tile(3 * d))
            bias_tab = _na_bias_table(na_rpb[i_na])
            y = _na_attention(qkv, bias_tab, batch, seq, d)
            w_out = na_w_out[i_na].astype(BF16)
            i_na += 1
        else:
            hk = dk // GLA_HEADS
            w_main = gla_w_in[i_gla][:, :3 * d]
            w_a = gla_w_in[i_gla][:, 3 * d:]
            w_in = jnp.concatenate([w_main, w_a, jnp.zeros((d, 128 - 2 * GLA_GATE_RANK), F32)],
                                   axis=1).astype(BF16)
            colscale = jnp.concatenate([jnp.full((dk,), hk ** -0.5, F32),
                                        jnp.ones((3 * d + 128 - dk,), F32)])
            proj = _norm_mod_matmul(xf, norm1_g[i], sc1, sh1, w_in, colscale, seq, _proj_tile(3 * d + 128))
            wg_pad = jnp.zeros((2, 128, dk), F32)
            wg_pad = wg_pad.at[0, :GLA_GATE_RANK].set(gla_w_gate_up[i_gla, 0])
            wg_pad = wg_pad.at[1, GLA_GATE_RANK:2 * GLA_GATE_RANK].set(gla_w_gate_up[i_gla, 1])
            y = _gla(proj, wg_pad.astype(BF16), gla_b_gate[i_gla], gla_gn_g[i_gla], batch, seq, d)
            w_out = gla_w_out[i_gla].astype(BF16)
            i_gla += 1
        xf = _matmul_residual(y, w_out, xf, gt1, seq)

        w_router = jnp.concatenate(
            [moe_w_router_group[i], moe_w_router_expert[i],
             jnp.zeros((d, ROUTER_LANES - MOE_GROUPS - MOE_EXPERTS), F32)], axis=1).astype(BF16)
        h2, eid, route, cnt = _router(xf, norm2_g[i], sc2, sh2, w_router, seq)
        block_expert, nvalid, d0, d1, pad_lo, pad_hi = _dispatch_tables(eid, cnt)
        yb = _moe_experts(h2, block_expert, d0, d1, nvalid, pad_lo, pad_hi, moe_w_up, moe_w_down, i)
        xf = _combine(yb, d0, d1, xf, route, gt2, seq)

    return _final_norm(xf, final_g).reshape(batch, seq, d)
```

```python
import functools

import jax
import jax.numpy as jnp
import numpy as np
from jax import lax
from jax.experimental import pallas as pl
from jax.experimental.pallas import tpu as pltpu

F32 = jnp.float32
BF16 = jnp.bfloat16

EPS = 1e-6
GRID_W = 64
NA_HEAD_DIM = 32
NA_WIN_ROWS = 8
NA_WIN_COLS = 16
NA_HEADS_PER_CHUNK = 4
NA_QBLOCKS = ((0, 24, 0), (24, 16, 16), (40, 24, 32))
NA_KEY_COLS = 32
NA_ROWS_PER_STEP = 16
MASK_VALUE = -1e30

GLA_HEADS = 4
GLA_GATE_RANK = 16
GLA_GATE_NORM = 16.0
GLA_CHUNK = 64
GLA_CHUNKS_PER_STEP = 8

MOE_GROUPS = 4
MOE_EXPERTS_PER_GROUP = 8
MOE_EXPERTS = MOE_GROUPS * MOE_EXPERTS_PER_GROUP
MOE_BLOCK = 256
MOE_GATHER_DEPTH = 3
MOE_FILL_UNROLL = 8
ROUTER_LANES = 128

VMEM_LIMIT = 56 * 1024 * 1024


def _cparams(sem):
    return pltpu.CompilerParams(dimension_semantics=sem, vmem_limit_bytes=VMEM_LIMIT)


def _silu(v):
    return v * jax.nn.sigmoid(v)


def _adaln_kernel(c_ref, w_ref, b_ref, o_ref):
    cond = _silu(c_ref[...]).astype(BF16)
    w = w_ref[...].astype(BF16)
    o_ref[...] = jnp.dot(cond, w, preferred_element_type=F32) + b_ref[...]


def _adaln(c, ada_w, ada_b):
    depth, d, n6 = ada_w.shape
    b = c.shape[0]
    rows = 16
    cpad = jnp.zeros((rows, d), F32).at[:b].set(c)
    tn = 1024
    out = pl.pallas_call(
        _adaln_kernel,
        grid=(depth, n6 // tn),
        in_specs=[
            pl.BlockSpec((rows, d), lambda i, j: (0, 0)),
            pl.BlockSpec((None, d, tn), lambda i, j: (i, 0, j)),
            pl.BlockSpec((None, 1, tn), lambda i, j: (i, 0, j)),
        ],
        out_specs=pl.BlockSpec((None, rows, tn), lambda i, j: (i, 0, j)),
        out_shape=jax.ShapeDtypeStruct((depth, rows, n6), F32),
        compiler_params=_cparams(("arbitrary", "arbitrary")),
        name="adaln",
    )(cpad, ada_w, ada_b.reshape(depth, 1, n6))
    return out[:, :b]


def _norm_mod(x, g, sc, sh):
    ms = jnp.mean(x * x, axis=-1, keepdims=True)
    y = x * lax.rsqrt(ms + EPS) * g
    return y * (1.0 + sc) + sh


def _nmm_kernel(x_ref, g_ref, sc_ref, sh_ref, w_ref, cs_ref, o_ref, h_ref):
    @pl.when(pl.program_id(1) == 0)
    def _():
        h_ref[...] = _norm_mod(x_ref[...], g_ref[...], sc_ref[...], sh_ref[...]).astype(BF16)

    acc = jnp.dot(h_ref[...], w_ref[...], preferred_element_type=F32)
    o_ref[...] = (acc * cs_ref[...]).astype(o_ref.dtype)


def _norm_mod_matmul(x, g, sc, sh, w, colscale, seq, tn):
    n, d = x.shape
    nout = w.shape[1]
    tm = min(1024, seq)
    return pl.pallas_call(
        _nmm_kernel,
        grid=(n // tm, nout // tn),
        in_specs=[
            pl.BlockSpec((tm, d), lambda i, j: (i, 0)),
            pl.BlockSpec((1, d), lambda i, j: (0, 0)),
            pl.BlockSpec((None, 1, d), lambda i, j: (i * tm // seq, 0, 0)),
            pl.BlockSpec((None, 1, d), lambda i, j: (i * tm // seq, 0, 0)),
            pl.BlockSpec((d, tn), lambda i, j: (0, j)),
            pl.BlockSpec((1, tn), lambda i, j: (0, j)),
        ],
        out_specs=pl.BlockSpec((tm, tn), lambda i, j: (i, j)),
        out_shape=jax.ShapeDtypeStruct((n, nout), BF16),
        scratch_shapes=[pltpu.VMEM((tm, d), BF16)],
        compiler_params=_cparams(("arbitrary", "arbitrary")),
        name="norm_mod_matmul",
    )(x, g.reshape(1, d), sc, sh, w, colscale.reshape(1, nout))


def _mmres_kernel(a_ref, w_ref, res_ref, gate_ref, o_ref):
    acc = jnp.dot(a_ref[...], w_ref[...], preferred_element_type=F32)
    o_ref[...] = res_ref[...] + gate_ref[...] * acc


def _matmul_residual(a, w, res, gate, seq):
    n, k = a.shape
    d = w.shape[1]
    tm = min(512, seq)
    return pl.pallas_call(
        _mmres_kernel,
        grid=(n // tm,),
        in_specs=[
            pl.BlockSpec((tm, k), lambda i: (i, 0)),
            pl.BlockSpec((k, d), lambda i: (0, 0)),
            pl.BlockSpec((tm, d), lambda i: (i, 0)),
            pl.BlockSpec((None, 1, d), lambda i: (i * tm // seq, 0, 0)),
        ],
        out_specs=pl.BlockSpec((tm, d), lambda i: (i, 0)),
        out_shape=jax.ShapeDtypeStruct((n, d), F32),
        compiler_params=_cparams(("arbitrary",)),
        name="matmul_residual",
    )(a, w, res, gate)


def _na_bias_table(rpb):
    heads = rpb.shape[0]
    chunks = heads // NA_HEADS_PER_CHUNK
    wr, wc = NA_WIN_ROWS, NA_WIN_COLS
    rpb = rpb.astype(F32)
    a = jnp.stack([rpb[:, wr - 1 - di:2 * wr - 1 - di, :] for di in range(wr)], axis=1)
    blocks = []
    for qs, nq, ks in NA_QBLOCKS:
        cols = []
        for q in range(qs, qs + nq):
            cstart = min(max(q - wc // 2, 0), GRID_W - wc)
            first = cstart - q + wc - 1
            off = cstart - ks
            cols.append(jnp.pad(a[..., first:first + wc],
                                ((0, 0), (0, 0), (0, 0), (off, NA_KEY_COLS - wc - off)),
                                constant_values=MASK_VALUE))
        t = jnp.stack(cols, axis=3)
        t = t.reshape(chunks, NA_HEADS_PER_CHUNK, wr, wr, nq, NA_KEY_COLS)
        t = t.transpose(0, 2, 1, 4, 3, 5)
        blocks.append(t.reshape(chunks, wr, NA_HEADS_PER_CHUNK * nq, wr * NA_KEY_COLS))
    return jnp.concatenate(blocks, axis=2)


def _na_kernel(q_ref, k_ref, v_ref, bias_ref, o_ref, l_ref, m_ref, p_ref, *, n_rows):
    rblk = pl.program_id(2)
    lane = lax.broadcasted_iota(jnp.int32, (1, 128), 1)
    hmask = [(lane // NA_HEAD_DIM == hp).astype(F32) for hp in range(NA_HEADS_PER_CHUNK)]
    win_tokens = NA_WIN_ROWS * GRID_W
    nkeys = NA_WIN_ROWS * NA_KEY_COLS
    ones = jnp.ones((nkeys, 128), BF16)

    def window(rr):
        r = rblk * NA_ROWS_PER_STEP + rr
        rs = jnp.clip(r - NA_WIN_ROWS // 2, 0, n_rows - NA_WIN_ROWS)
        return pl.multiple_of(rs * GRID_W, GRID_W), r - rs

    def key_block(win, ks):
        return jnp.concatenate([win[i * GRID_W + ks:i * GRID_W + ks + NA_KEY_COLS]
                                for i in range(NA_WIN_ROWS)], axis=0)

    for rr in range(NA_ROWS_PER_STEP):
        w0, di = window(rr)
        qrow = q_ref[pl.ds(rr * GRID_W, GRID_W), :].astype(F32)
        kwin = k_ref[pl.ds(w0, win_tokens), :]
        row0 = 0
        for qs, nq, ks in NA_QBLOCKS:
            nst = NA_HEADS_PER_CHUNK * nq
            qb = qrow[qs:qs + nq]
            qstack = jnp.concatenate([qb * hmask[hp] for hp in range(NA_HEADS_PER_CHUNK)],
                                     axis=0).astype(BF16)
            logits = lax.dot_general(qstack, key_block(kwin, ks), (((1,), (1,)), ((), ())),
                                     preferred_element_type=F32)
            l_ref[rr, pl.ds(row0, nst), :] = logits + bias_ref[di, pl.ds(row0, nst), :]
            row0 += nst

    for rr in range(NA_ROWS_PER_STEP):
        m = jnp.max(l_ref[rr], axis=-1, keepdims=True)
        m_ref[rr] = jnp.broadcast_to(m, m_ref.shape[1:])

    for rr in range(NA_ROWS_PER_STEP):
        m = m_ref[rr]
        p_ref[rr] = jnp.exp(l_ref[rr] - jnp.concatenate([m, m], axis=1)).astype(BF16)

    for rr in range(NA_ROWS_PER_STEP):
        w0, _ = window(rr)
        vwin = v_ref[pl.ds(w0, win_tokens), :]
        outs = []
        row0 = 0
        for qs, nq, ks in NA_QBLOCKS:
            nst = NA_HEADS_PER_CHUNK * nq
            v1 = jnp.concatenate([key_block(vwin, ks), ones], axis=1)
            acc = jnp.dot(p_ref[rr, pl.ds(row0, nst), :], v1, preferred_element_type=F32)
            num = acc[0:nq, :128] * hmask[0]
            den = acc[0:nq, 128:] * hmask[0]
            for hp in range(1, NA_HEADS_PER_CHUNK):
                num = num + acc[hp * nq:(hp + 1) * nq, :128] * hmask[hp]
                den = den + acc[hp * nq:(hp + 1) * nq, 128:] * hmask[hp]
            outs.append(num * (1.0 / den))
            row0 += nst
        orow = jnp.concatenate(outs, axis=0)
        o_ref[pl.ds(rr * GRID_W, GRID_W), :] = orow.astype(o_ref.dtype)


def _na_attention(qkv, bias_tab, batch, seq, d):
    n = qkv.shape[0]
    n_rows = seq // GRID_W
    chunks = d // 128
    step_tokens = NA_ROWS_PER_STEP * GRID_W
    rsteps = n_rows // NA_ROWS_PER_STEP
    nrow = bias_tab.shape[2]
    nkey = bias_tab.shape[3]
    return pl.pallas_call(
        functools.partial(_na_kernel, n_rows=n_rows),
        grid=(batch, chunks, rsteps),
        in_specs=[
            pl.BlockSpec((step_tokens, 128), lambda b, c, r: (b * rsteps + r, c)),
            pl.BlockSpec((seq, 128), lambda b, c, r: (b, chunks + c)),
            pl.BlockSpec((seq, 128), lambda b, c, r: (b, 2 * chunks + c)),
            pl.BlockSpec((None, NA_WIN_ROWS, nrow, nkey), lambda b, c, r: (c, 0, 0, 0)),
        ],
        out_specs=pl.BlockSpec((step_tokens, 128), lambda b, c, r: (b * rsteps + r, c)),
        out_shape=jax.ShapeDtypeStruct((n, d), BF16),
        scratch_shapes=[
            pltpu.VMEM((NA_ROWS_PER_STEP, nrow, nkey), F32),
            pltpu.VMEM((NA_ROWS_PER_STEP, nrow, 128), F32),
            pltpu.VMEM((NA_ROWS_PER_STEP, nrow, nkey), BF16),
        ],
        compiler_params=_cparams(("arbitrary", "arbitrary", "arbitrary")),
        name="na_attention",
    )(qkv, qkv, qkv, bias_tab)


def _log_sigmoid(v):
    return jnp.minimum(v, 0.0) - jnp.log(1.0 + jnp.exp(-jnp.abs(v)))


def _gla_kernel(q_ref, k_ref, v_ref, r_ref, a_ref, wg_ref, bg_ref, gn_ref, o_ref,
                st_ref, of_ref, cum_ref, qe_ref, ke_ref, kdt_ref, att_ref, av_ref, u_ref, dec_ref, os_ref,
                *, cb, nb):
    p = pl.program_id(2)
    i = pl.program_id(3)
    fwd = p == 0
    step_rows = cb * GLA_CHUNK

    @pl.when(i == 0)
    def _():
        st_ref[...] = jnp.zeros_like(st_ref)

    blk = jnp.where(fwd, i, nb - 1 - i)
    tr = lax.broadcasted_iota(jnp.int32, (GLA_CHUNK, GLA_CHUNK), 0)
    tc = lax.broadcasted_iota(jnp.int32, (GLA_CHUNK, GLA_CHUNK), 1)
    tmask = (tr - tc) * (1 - 2 * p) >= 0
    tmat = tmask.astype(BF16)
    nt = (((1,), (1,)), ((), ()))

    gl = jnp.dot(a_ref[...], wg_ref[...], preferred_element_type=F32) + bg_ref[...]
    g = _log_sigmoid(gl) * (1.0 / GLA_GATE_NORM)
    g1 = g.astype(BF16)
    e1 = g - g1.astype(F32)
    g2 = e1.astype(BF16)
    g3 = (e1 - g2.astype(F32)).astype(BF16)
    for c in range(cb):
        rows = slice(c * GLA_CHUNK, (c + 1) * GLA_CHUNK)
        cum_ref[rows, :] = (jnp.dot(tmat, g1[rows], preferred_element_type=F32)
                            + jnp.dot(tmat, g2[rows], preferred_element_type=F32)
                            + jnp.dot(tmat, g3[rows], preferred_element_type=F32))
    for c in range(cb):
        rows = slice(c * GLA_CHUNK, (c + 1) * GLA_CHUNK)
        cum = cum_ref[rows, :]
        last = jnp.where(fwd, cum[GLA_CHUNK - 1:GLA_CHUNK], cum[0:1])
        q = q_ref[rows, :].astype(F32)
        k = k_ref[rows, :].astype(F32)
        qe_ref[c] = (q * jnp.exp(cum)).astype(BF16)
        ke_ref[c] = (k * jnp.exp(-cum)).astype(BF16)
        kdt_ref[c] = (k * jnp.exp(last - cum)).T.astype(BF16)
        dcol = jnp.broadcast_to(jnp.exp(last), (8, last.shape[1])).T
        dec_ref[c] = jnp.broadcast_to(dcol[:, 0:1], dec_ref.shape[1:])
    for c in range(cb):
        att = lax.dot_general(qe_ref[c], ke_ref[c], nt, preferred_element_type=F32)
        att_ref[c] = jnp.where(tmask, att, 0.0).astype(BF16)
    for c in range(cb):
        rows = slice(c * GLA_CHUNK, (c + 1) * GLA_CHUNK)
        v = v_ref[rows, :]
        av_ref[c] = jnp.dot(att_ref[c], v, preferred_element_type=F32)
        u_ref[c] = jnp.dot(kdt_ref[c], v, preferred_element_type=F32)

    lane_tiles = st_ref.shape[1] // 128
    for cc in range(cb):
        c = jnp.where(fwd, cc, cb - 1 - cc)
        st = st_ref[...]
        o = av_ref[c] + jnp.dot(qe_ref[c], st.astype(BF16), preferred_element_type=F32)
        dec = dec_ref[c]
        st_ref[...] = st * jnp.concatenate([dec] * lane_tiles, axis=1) + u_ref[c]
        os_ref[pl.ds(pl.multiple_of(c * GLA_CHUNK, GLA_CHUNK), GLA_CHUNK), :] = o

    g0 = pl.multiple_of(blk * step_rows, step_rows)

    @pl.when(fwd)
    def _():
        of_ref[pl.ds(g0, step_rows), :] = os_ref[...]

    @pl.when(jnp.logical_not(fwd))
    def _():
        ot = of_ref[pl.ds(g0, step_rows), :] + os_ref[...]
        ms = jnp.mean(ot * ot, axis=-1, keepdims=True)
        y = ot * lax.rsqrt(ms + EPS) * gn_ref[...]
        y = y * _silu(r_ref[...].astype(F32))
        o_ref[...] = y.astype(o_ref.dtype)


def _gla(proj, a_low, wg_pad, b_gate, gn_g, batch, seq, d):
    n = proj.shape[0]
    hk = d // 2 // GLA_HEADS
    hv = d // GLA_HEADS
    cb = min(GLA_CHUNKS_PER_STEP, seq // GLA_CHUNK)
    step_rows = cb * GLA_CHUNK
    nb = seq // step_rows

    def rowblk(b, p, i):
        return b * nb + jnp.where(p == 0, i, nb - 1 - i)

    def outblk(b, p, i):
        return b * nb + jnp.where(p == 0, nb - 1, nb - 1 - i)

    return pl.pallas_call(
        functools.partial(_gla_kernel, cb=cb, nb=nb),
        grid=(batch, GLA_HEADS, 2, nb),
        in_specs=[
            pl.BlockSpec((step_rows, hk), lambda b, h, p, i: (rowblk(b, p, i), h)),
            pl.BlockSpec((step_rows, hk), lambda b, h, p, i: (rowblk(b, p, i), GLA_HEADS + h)),
            pl.BlockSpec((step_rows, hv), lambda b, h, p, i: (rowblk(b, p, i), GLA_HEADS + h)),
            pl.BlockSpec((step_rows, hv), lambda b, h, p, i: (rowblk(b, p, i), 2 * GLA_HEADS + h)),
            pl.BlockSpec((step_rows, 128), lambda b, h, p, i: (rowblk(b, p, i), 0)),
            pl.BlockSpec((None, 128, hk), lambda b, h, p, i: (p, 0, h)),
            pl.BlockSpec((None, 1, hk), lambda b, h, p, i: (p, 0, h)),
            pl.BlockSpec((1, hv), lambda b, h, p, i: (0, h)),
        ],
        out_specs=pl.BlockSpec((step_rows, hv), lambda b, h, p, i: (outblk(b, p, i), h)),
        out_shape=jax.ShapeDtypeStruct((n, d), BF16),
        scratch_shapes=[
            pltpu.VMEM((hk, hv), F32),
            pltpu.VMEM((seq, hv), F32),
            pltpu.VMEM((step_rows, hk), F32),
            pltpu.VMEM((cb, GLA_CHUNK, hk), BF16),
            pltpu.VMEM((cb, GLA_CHUNK, hk), BF16),
            pltpu.VMEM((cb, hk, GLA_CHUNK), BF16),
            pltpu.VMEM((cb, GLA_CHUNK, GLA_CHUNK), BF16),
            pltpu.VMEM((cb, GLA_CHUNK, hv), F32),
            pltpu.VMEM((cb, hk, hv), F32),
            pltpu.VMEM((cb, hk, 128), F32),
            pltpu.VMEM((step_rows, hv), F32),
        ],
        compiler_params=_cparams(("arbitrary", "arbitrary", "arbitrary", "arbitrary")),
        name="gla",
    )(proj, proj, proj, proj, a_low, wg_pad, b_gate.reshape(2, 1, -1), gn_g.reshape(1, d))


def _router_kernel(x_ref, g_ref, sc_ref, sh_ref, w_ref, h_ref, eid_ref, rt_ref, cnt_ref, run_ref):
    @pl.when(pl.program_id(0) == 0)
    def _():
        run_ref[...] = jnp.zeros_like(run_ref)

    h = _norm_mod(x_ref[...], g_ref[...], sc_ref[...], sh_ref[...])
    hb = h.astype(BF16)
    half = h.shape[1] // 2
    bits = lax.bitcast_convert_type(hb.astype(F32), jnp.uint32)
    h_ref[...] = (bits[:, :half] >> 16) | (bits[:, half:] & jnp.uint32(0xFFFF0000))
    logits = jnp.dot(hb, w_ref[...], preferred_element_type=F32)
    lane = lax.broadcasted_iota(jnp.int32, logits.shape, 1)
    gl = jnp.where(lane < MOE_GROUPS, logits, MASK_VALUE)
    gmax = jnp.max(gl, axis=-1, keepdims=True)
    gidx = jnp.min(jnp.where(gl == gmax, lane, ROUTER_LANES), axis=-1, keepdims=True)
    g_w = 1.0 / jnp.sum(jnp.exp(gl - gmax), axis=-1, keepdims=True)
    e_lane = lane - MOE_GROUPS
    in_group = (e_lane >= 0) & (e_lane < MOE_EXPERTS) & ((e_lane >> 3) == gidx)
    el = jnp.where(in_group, logits, MASK_VALUE)
    m1 = jnp.max(el, axis=-1, keepdims=True)
    i1 = jnp.min(jnp.where(el == m1, lane, ROUTER_LANES), axis=-1, keepdims=True)
    el2 = jnp.where(lane == i1, MASK_VALUE, el)
    m2 = jnp.max(el2, axis=-1, keepdims=True)
    i2 = jnp.min(jnp.where(el2 == m2, lane, ROUTER_LANES), axis=-1, keepdims=True)
    t = jnp.exp(m2 - m1)
    w1 = g_w / (1.0 + t)
    w2 = w1 * t
    rt_ref[...] = jnp.where(lane == 0, w1, jnp.where(lane == 1, w2, 0.0))
    tm = logits.shape[0]
    oh1 = lane == i1
    oh2 = lane == i2
    both = (oh1 | oh2).astype(BF16)
    tr = lax.broadcasted_iota(jnp.int32, (tm, tm), 0)
    tc = lax.broadcasted_iota(jnp.int32, (tm, tm), 1)
    before = (tc < tr).astype(BF16)
    prior = jnp.dot(before, both, preferred_element_type=F32) + run_ref[0:1, :]
    r1 = jnp.sum(jnp.where(oh1, prior, 0.0), axis=-1, keepdims=True).astype(jnp.int32)
    r2 = jnp.sum(jnp.where(oh2, prior, 0.0), axis=-1, keepdims=True).astype(jnp.int32)
    eid_ref[...] = jnp.where(lane == 0, i1 - MOE_GROUPS,
                             jnp.where(lane == 1, i2 - MOE_GROUPS,
                                       jnp.where(lane == 2, r1, jnp.where(lane == 3, r2, 0))))
    run = run_ref[...] + jnp.sum(both.astype(F32), axis=0, keepdims=True)
    run_ref[...] = run
    cnt_ref[...] = run


def _router(x, g, sc, sh, w_router, seq):
    n, d = x.shape
    tm = min(512, seq)
    return pl.pallas_call(
        _router_kernel,
        grid=(n // tm,),
        in_specs=[
            pl.BlockSpec((tm, d), lambda i: (i, 0)),
            pl.BlockSpec((1, d), lambda i: (0, 0)),
            pl.BlockSpec((None, 1, d), lambda i: (i * tm // seq, 0, 0)),
            pl.BlockSpec((None, 1, d), lambda i: (i * tm // seq, 0, 0)),
            pl.BlockSpec((d, ROUTER_LANES), lambda i: (0, 0)),
        ],
        out_specs=[
            pl.BlockSpec((tm, d // 2), lambda i: (i, 0)),
            pl.BlockSpec((tm, ROUTER_LANES), lambda i: (i, 0)),
            pl.BlockSpec((tm, ROUTER_LANES), lambda i: (i, 0)),
            pl.BlockSpec((8, ROUTER_LANES), lambda i: (0, 0)),
        ],
        out_shape=[
            jax.ShapeDtypeStruct((n, d // 2), jnp.uint32),
            jax.ShapeDtypeStruct((n, ROUTER_LANES), jnp.int32),
            jax.ShapeDtypeStruct((n, ROUTER_LANES), F32),
            jax.ShapeDtypeStruct((8, ROUTER_LANES), F32),
        ],
        scratch_shapes=[pltpu.VMEM((8, ROUTER_LANES), F32)],
        compiler_params=_cparams(("arbitrary",)),
        name="moe_router",
    )(x, g.reshape(1, d), sc, sh, w_router)


def _dispatch_tables(eid, cnt):
    n = eid.shape[0]
    a = n * 2
    flat_e = eid[:, 0:2].reshape(a)
    rank = eid[:, 2:4].reshape(a)
    counts = cnt[0, MOE_GROUPS:MOE_GROUPS + MOE_EXPERTS].astype(jnp.int32)
    padded = (counts + MOE_BLOCK - 1) // MOE_BLOCK * MOE_BLOCK
    pend = jnp.cumsum(padded)
    pstart = pend - padded
    onehot = flat_e[:, None] == jnp.arange(MOE_EXPERTS, dtype=jnp.int32)[None, :]
    dest = jnp.sum(jnp.where(onehot, pstart[None, :], 0), axis=1) + rank
    n_blocks = (a + MOE_EXPERTS * MOE_BLOCK) // MOE_BLOCK
    bstart = jnp.arange(n_blocks, dtype=jnp.int32) * MOE_BLOCK
    block_expert = jnp.minimum(jnp.sum((pend[None, :] <= bstart[:, None]).astype(jnp.int32), axis=1),
                               MOE_EXPERTS - 1)
    nvalid = (pend[-1] // MOE_BLOCK).astype(jnp.int32).reshape(1)
    dest2 = dest.reshape(n, 2)
    return block_expert, nvalid, dest2[:, 0], dest2[:, 1], pstart + counts, pend


def _moe_kernel(be_ref, d0_ref, d1_ref, nv_ref, plo_ref, phi_ref, h_hbm, wup_ref, wdn_ref, o_ref,
                tok_ref, xbuf, xb_ref, wup_bf, wdn_bf, sem, *, tb, ff):
    b = pl.program_id(0)
    nvalid = nv_ref[0]
    n_tok = d0_ref.shape[0]

    def row_copy(tok, i, slot):
        return pltpu.make_async_copy(h_hbm.at[pl.ds(tok, 1), :], xbuf.at[slot, pl.ds(i, 1), :],
                                     sem.at[slot])

    def wait_rows(slot):
        pltpu.make_async_copy(h_hbm.at[pl.ds(0, tb), :], xbuf.at[slot], sem.at[slot]).wait()

    @pl.when(b == 0)
    def _():
        def fill(i, carry):
            for u in range(MOE_FILL_UNROLL):
                t = i * MOE_FILL_UNROLL + u
                tok_ref[d0_ref[t]] = t
                tok_ref[d1_ref[t]] = t
            return carry
        lax.fori_loop(0, n_tok // MOE_FILL_UNROLL, fill, 0)

        def pad_expert(e, carry):
            def pad(s, c):
                tok_ref[s] = 0
                return c
            lax.fori_loop(plo_ref[e], phi_ref[e], pad, 0)
            return carry
        lax.fori_loop(0, MOE_EXPERTS, pad_expert, 0)

        for ahead in range(MOE_GATHER_DEPTH - 1):
            base = jnp.minimum(ahead, nvalid - 1) * tb

            def body(i, carry, base=base, ahead=ahead):
                row_copy(tok_ref[base + i], i, ahead).start()
                return carry
            lax.fori_loop(0, tb, body, 0)

    prev = be_ref[jnp.maximum(b - 1, 0)]

    @pl.when((b < nvalid) & ((b == 0) | (be_ref[b] != prev)))
    def _():
        wup_bf[...] = wup_ref[...].astype(BF16)
        wdn_bf[...] = wdn_ref[...].astype(BF16)

    @pl.when(b < nvalid)
    def _():
        slot = lax.rem(b, MOE_GATHER_DEPTH)
        wait_rows(slot)
        xp = xbuf[slot]
        half = xp.shape[1]
        xb_ref[:, :half] = lax.bitcast_convert_type(xp << 16, F32).astype(BF16)
        xb_ref[:, half:] = lax.bitcast_convert_type(xp & jnp.uint32(0xFFFF0000), F32).astype(BF16)
        ahead = MOE_GATHER_DEPTH - 1
        nslot = lax.rem(b + ahead, MOE_GATHER_DEPTH)
        base = jnp.minimum(b + ahead, nvalid - 1) * tb
        for i in range(tb):
            row_copy(tok_ref[base + i], i, nslot).start()
        hcat = jnp.dot(xb_ref[...], wup_bf[...], preferred_element_type=F32)
        act = (_silu(hcat[:, :ff]) * hcat[:, ff:]).astype(BF16)
        o_ref[...] = jnp.dot(act, wdn_bf[...], preferred_element_type=F32)

        @pl.when(b == nvalid - 1)
        def _():
            for k in range(1, MOE_GATHER_DEPTH):
                wait_rows(lax.rem(b + k, MOE_GATHER_DEPTH))

    @pl.when(b >= nvalid)
    def _():
        o_ref[...] = jnp.zeros_like(o_ref)


def _moe_experts(h2, block_expert, d0, d1, nvalid, pad_lo, pad_hi, w_up, w_down, layer):
    n = h2.shape[0]
    d = 2 * h2.shape[1]
    tb = MOE_BLOCK
    n_blocks = block_expert.shape[0]
    n_slots = n_blocks * tb
    ff = w_down.shape[2]
    grid_spec = pltpu.PrefetchScalarGridSpec(
        num_scalar_prefetch=6,
        grid=(n_blocks,),
        in_specs=[
            pl.BlockSpec(memory_space=pl.ANY),
            pl.BlockSpec((None, None, d, 2 * ff), lambda b, be, *_: (layer, be[b], 0, 0)),
            pl.BlockSpec((None, None, ff, d), lambda b, be, *_: (layer, be[b], 0, 0)),
        ],
        out_specs=pl.BlockSpec((tb, d), lambda b, be, *_: (b, 0)),
        scratch_shapes=[
            pltpu.SMEM((n_slots,), jnp.int32),
            pltpu.VMEM((MOE_GATHER_DEPTH, tb, d // 2), jnp.uint32),
            pltpu.VMEM((tb, d), BF16),
            pltpu.VMEM((d, 2 * ff), BF16),
            pltpu.VMEM((ff, d), BF16),
            pltpu.SemaphoreType.DMA((MOE_GATHER_DEPTH,)),
        ],
    )
    return pl.pallas_call(
        functools.partial(_moe_kernel, tb=tb, ff=ff),
        grid_spec=grid_spec,
        out_shape=jax.ShapeDtypeStruct((n_slots, d), F32),
        compiler_params=_cparams(("arbitrary",)),
        name="moe_experts",
    )(block_expert, d0, d1, nvalid, pad_lo, pad_hi, h2, w_up, w_down)


def _combine_kernel(d0_ref, d1_ref, yb_hbm, x_ref, rt_ref, gate_ref, *rest, tm, tail):
    if tail == "codes":
        g_ref, sc_ref, sh_ref, wa_ref, o_ref, a_ref, ybuf, sem = rest
    elif tail == "final":
        g_ref, o_ref, ybuf, sem = rest
    else:
        o_ref, ybuf, sem = rest
    t = pl.program_id(0)
    nt = pl.num_programs(0)

    def row_copy(src_row, i, slot):
        return pltpu.make_async_copy(yb_hbm.at[pl.ds(src_row, 1), :], ybuf.at[slot, pl.ds(i, 1), :],
                                     sem.at[slot])

    def wait_rows(slot):
        pltpu.make_async_copy(yb_hbm.at[pl.ds(0, 2 * tm), :], ybuf.at[slot], sem.at[slot]).wait()

    def gather(blk, slot):
        base = blk * tm
        for i in range(tm):
            row_copy(d0_ref[base + i], i, slot).start()
            row_copy(d1_ref[base + i], tm + i, slot).start()

    ahead = MOE_GATHER_DEPTH - 1

    @pl.when(t == 0)
    def _():
        for k in range(ahead):
            gather(jnp.minimum(k, nt - 1), k)

    slot = lax.rem(t, MOE_GATHER_DEPTH)
    wait_rows(slot)
    gather(jnp.minimum(t + ahead, nt - 1), lax.rem(t + ahead, MOE_GATHER_DEPTH))
    rt = rt_ref[...]
    y = rt[:, 0:1] * ybuf[slot, pl.ds(0, tm), :] + rt[:, 1:2] * ybuf[slot, pl.ds(tm, tm), :]
    xn = x_ref[...] + gate_ref[...] * y
    if tail == "final":
        ms = jnp.mean(xn * xn, axis=-1, keepdims=True)
        o_ref[...] = xn * lax.rsqrt(ms + EPS) * g_ref[...]
    else:
        o_ref[...] = xn
    if tail == "codes":
        h = _norm_mod(xn, g_ref[...], sc_ref[...], sh_ref[...]).astype(BF16)
        a_ref[...] = jnp.dot(h, wa_ref[...], preferred_element_type=F32).astype(a_ref.dtype)

    @pl.when(t == nt - 1)
    def _():
        for k in range(1, MOE_GATHER_DEPTH):
            wait_rows(lax.rem(t + k, MOE_GATHER_DEPTH))


def _combine(yb, d0, d1, x, route, gate, seq, tail="plain", norm=None):
    n, d = x.shape
    tm = min(256, seq)
    row = lambda t, d0, d1: (t, 0)
    const = lambda t, d0, d1: (0, 0)
    per_batch = lambda t, d0, d1: (t * tm // seq, 0, 0)
    in_specs = [
        pl.BlockSpec(memory_space=pl.ANY),
        pl.BlockSpec((tm, d), row),
        pl.BlockSpec((tm, ROUTER_LANES), row),
        pl.BlockSpec((None, 1, d), per_batch),
    ]
    args = [yb, x, route, gate]
    out_specs = pl.BlockSpec((tm, d), row)
    out_shape = jax.ShapeDtypeStruct((n, d), F32)
    if tail == "codes":
        g, sc, sh, w_a = norm
        in_specs += [pl.BlockSpec((1, d), const), pl.BlockSpec((None, 1, d), per_batch),
                     pl.BlockSpec((None, 1, d), per_batch), pl.BlockSpec((d, 128), const)]
        args += [g.reshape(1, d), sc, sh, w_a]
        out_specs = [out_specs, pl.BlockSpec((tm, 128), row)]
        out_shape = [out_shape, jax.ShapeDtypeStruct((n, 128), BF16)]
    elif tail == "final":
        in_specs += [pl.BlockSpec((1, d), const)]
        args += [norm[0].reshape(1, d)]
    grid_spec = pltpu.PrefetchScalarGridSpec(
        num_scalar_prefetch=2,
        grid=(n // tm,),
        in_specs=in_specs,
        out_specs=out_specs,
        scratch_shapes=[pltpu.VMEM((MOE_GATHER_DEPTH, 2 * tm, d), F32),
                        pltpu.SemaphoreType.DMA((MOE_GATHER_DEPTH,))],
    )
    return pl.pallas_call(
        functools.partial(_combine_kernel, tm=tm, tail=tail),
        grid_spec=grid_spec,
        out_shape=out_shape,
        compiler_params=_cparams(("arbitrary",)),
        name="moe_combine",
    )(d0, d1, *args)


def _proj_tile(nout):
    for tn in (1536, 896, 768, 512, 256, 128):
        if nout % tn == 0:
            return tn
    raise ValueError(f"no projection tile for width {nout}")


def kernel(x, c, ada_w, ada_b, norm1_g, norm2_g, na_w_in, na_w_out, na_rpb, gla_w_in, gla_w_gate_up, gla_b_gate, gla_gn_g, gla_w_out, moe_w_router_group, moe_w_router_expert, moe_w_up, moe_w_down, final_g):
    batch, seq, d = x.shape
    depth = ada_w.shape[0]
    n = batch * seq
    assert seq % (GRID_W * NA_ROWS_PER_STEP) == 0 and d % 1024 == 0
    dk = d // 2

    mod = _adaln(c, ada_w, ada_b)
    xf = x.reshape(n, d)
    i_na = 0
    i_gla = 0
    for i in range(depth):
        sh1, sc1, gt1, sh2, sc2, gt2 = [m.reshape(batch, 1, d) for m in jnp.split(mod[i], 6, axis=-1)]
        if i % 2 == 0:
            w_in = na_w_in[i_na].astype(BF16)
            colscale = jnp.concatenate([jnp.full((d,), NA_HEAD_DIM ** -0.5, F32), jnp.ones((2 * d,), F32)])
            qkv = _norm_mod_matmul(xf, norm1_g[i], sc1, sh1, w_in, colscale, seq, _proj_tile(3 * d))
            bias_tab = _na_bias_table(na_rpb[i_na])
            y = _na_attention(qkv, bias_tab, batch, seq, d)
            w_out = na_w_out[i_na].astype(BF16)
            i_na += 1
        else:
            hk = dk // GLA_HEADS
            w_in = gla_w_in[i_gla][:, :3 * d].astype(BF16)
            colscale = jnp.concatenate([jnp.full((dk,), hk ** -0.5, F32), jnp.ones((3 * d - dk,), F32)])
            proj = _norm_mod_matmul(xf, norm1_g[i], sc1, sh1, w_in, colscale, seq, _proj_tile(3 * d))
            wg_pad = jnp.zeros((2, 128, dk), F32)
            wg_pad = wg_pad.at[0, :GLA_GATE_RANK].set(gla_w_gate_up[i_gla, 0])
            wg_pad = wg_pad.at[1, GLA_GATE_RANK:2 * GLA_GATE_RANK].set(gla_w_gate_up[i_gla, 1])
            y = _gla(proj, a_low, wg_pad.astype(BF16), gla_b_gate[i_gla], gla_gn_g[i_gla], batch, seq, d)
            w_out = gla_w_out[i_gla].astype(BF16)
            i_gla += 1
        xf = _matmul_residual(y, w_out, xf, gt1, seq)

        w_router = jnp.concatenate(
            [moe_w_router_group[i], moe_w_router_expert[i],
             jnp.zeros((d, ROUTER_LANES - MOE_GROUPS - MOE_EXPERTS), F32)], axis=1).astype(BF16)
        h2, eid, route, cnt = _router(xf, norm2_g[i], sc2, sh2, w_router, seq)
        block_expert, nvalid, d0, d1, pad_lo, pad_hi = _dispatch_tables(eid, cnt)
        yb = _moe_experts(h2, block_expert, d0, d1, nvalid, pad_lo, pad_hi, moe_w_up, moe_w_down, i)
        if i == depth - 1:
            xf = _combine(yb, d0, d1, xf, route, gt2, seq, "final", (final_g,))
        elif (i + 1) % 2 == 1:
            nsh1, nsc1 = [m.reshape(batch, 1, d) for m in jnp.split(mod[i + 1], 6, axis=-1)[:2]]
            w_a = jnp.concatenate([gla_w_in[i_gla][:, 3 * d:],
                                   jnp.zeros((d, 128 - 2 * GLA_GATE_RANK), F32)], axis=1).astype(BF16)
            xf, a_low = _combine(yb, d0, d1, xf, route, gt2, seq, "codes",
                                 (norm1_g[i + 1], nsc1, nsh1, w_a))
        else:
            xf = _combine(yb, d0, d1, xf, route, gt2, seq)

    return xf.reshape(batch, seq, d)
```

```python
import functools

import jax
import jax.numpy as jnp
import numpy as np
from jax import lax
from jax.experimental import pallas as pl
from jax.experimental.pallas import tpu as pltpu

F32 = jnp.float32
BF16 = jnp.bfloat16

EPS = 1e-6
GRID_W = 64
NA_HEAD_DIM = 32
NA_WIN_ROWS = 8
NA_WIN_COLS = 16
NA_HEADS_PER_CHUNK = 4
NA_QBLOCKS = ((0, 24, 0), (24, 16, 16), (40, 24, 32))
NA_KEY_COLS = 32
NA_ROWS_PER_STEP = 32
NA_GROUP_ROWS = 32
MASK_VALUE = -1e30

GLA_HEADS = 4
GLA_GATE_RANK = 16
GLA_GATE_NORM = 16.0
GLA_CHUNK = 64
GLA_CHUNKS_PER_STEP = 16

MOE_GROUPS = 4
MOE_EXPERTS_PER_GROUP = 8
MOE_EXPERTS = MOE_GROUPS * MOE_EXPERTS_PER_GROUP
MOE_BLOCK = 256
MOE_GATHER_DEPTH = 3
MOE_FILL_UNROLL = 8
ROUTER_LANES = 128

VMEM_LIMIT = 56 * 1024 * 1024


def _cparams(sem):
    return pltpu.CompilerParams(dimension_semantics=sem, vmem_limit_bytes=VMEM_LIMIT)


def _silu(v):
    return v * jax.nn.sigmoid(v)


def _adaln_kernel(c_ref, w_ref, b_ref, o_ref):
    cond = _silu(c_ref[...]).astype(BF16)
    w = w_ref[...].astype(BF16)
    o_ref[...] = jnp.dot(cond, w, preferred_element_type=F32) + b_ref[...]


def _adaln(c, ada_w, ada_b):
    depth, d, n6 = ada_w.shape
    b = c.shape[0]
    rows = 16
    cpad = jnp.zeros((rows, d), F32).at[:b].set(c)
    tn = 1024
    out = pl.pallas_call(
        _adaln_kernel,
        grid=(depth, n6 // tn),
        in_specs=[
            pl.BlockSpec((rows, d), lambda i, j: (0, 0)),
            pl.BlockSpec((None, d, tn), lambda i, j: (i, 0, j)),
            pl.BlockSpec((None, 1, tn), lambda i, j: (i, 0, j)),
        ],
        out_specs=pl.BlockSpec((None, rows, tn), lambda i, j: (i, 0, j)),
        out_shape=jax.ShapeDtypeStruct((depth, rows, n6), F32),
        compiler_params=_cparams(("arbitrary", "arbitrary")),
        name="adaln",
    )(cpad, ada_w, ada_b.reshape(depth, 1, n6))
    return out[:, :b]


def _norm_mod(x, g, sc, sh):
    ms = jnp.mean(x * x, axis=-1, keepdims=True)
    y = x * lax.rsqrt(ms + EPS) * g
    return y * (1.0 + sc) + sh


def _nmm_kernel(x_ref, g_ref, sc_ref, sh_ref, w_ref, cs_ref, o_ref, h_ref):
    @pl.when(pl.program_id(1) == 0)
    def _():
        h_ref[...] = _norm_mod(x_ref[...], g_ref[...], sc_ref[...], sh_ref[...]).astype(BF16)

    acc = jnp.dot(h_ref[...], w_ref[...], preferred_element_type=F32)
    o_ref[...] = (acc * cs_ref[...]).astype(o_ref.dtype)


def _norm_mod_matmul(x, g, sc, sh, w, colscale, seq, tn):
    n, d = x.shape
    nout = w.shape[1]
    tm = min(1024, seq)
    return pl.pallas_call(
        _nmm_kernel,
        grid=(n // tm, nout // tn),
        in_specs=[
            pl.BlockSpec((tm, d), lambda i, j: (i, 0)),
            pl.BlockSpec((1, d), lambda i, j: (0, 0)),
            pl.BlockSpec((None, 1, d), lambda i, j: (i * tm // seq, 0, 0)),
            pl.BlockSpec((None, 1, d), lambda i, j: (i * tm // seq, 0, 0)),
            pl.BlockSpec((d, tn), lambda i, j: (0, j)),
            pl.BlockSpec((1, tn), lambda i, j: (0, j)),
        ],
        out_specs=pl.BlockSpec((tm, tn), lambda i, j: (i, j)),
        out_shape=jax.ShapeDtypeStruct((n, nout), BF16),
        scratch_shapes=[pltpu.VMEM((tm, d), BF16)],
        compiler_params=_cparams(("arbitrary", "arbitrary")),
        name="norm_mod_matmul",
    )(x, g.reshape(1, d), sc, sh, w, colscale.reshape(1, nout))


def _mmres_kernel(a_ref, w_ref, res_ref, gate_ref, o_ref):
    acc = jnp.dot(a_ref[...], w_ref[...], preferred_element_type=F32)
    o_ref[...] = res_ref[...] + gate_ref[...] * acc


def _matmul_residual(a, w, res, gate, seq):
    n, k = a.shape
    d = w.shape[1]
    tm = min(512, seq)
    return pl.pallas_call(
        _mmres_kernel,
        grid=(n // tm,),
        in_specs=[
            pl.BlockSpec((tm, k), lambda i: (i, 0)),
            pl.BlockSpec((k, d), lambda i: (0, 0)),
            pl.BlockSpec((tm, d), lambda i: (i, 0)),
            pl.BlockSpec((None, 1, d), lambda i: (i * tm // seq, 0, 0)),
        ],
        out_specs=pl.BlockSpec((tm, d), lambda i: (i, 0)),
        out_shape=jax.ShapeDtypeStruct((n, d), F32),
        compiler_params=_cparams(("arbitrary",)),
        name="matmul_residual",
    )(a, w, res, gate)


def _na_bias_table(rpb):
    heads = rpb.shape[0]
    chunks = heads // NA_HEADS_PER_CHUNK
    wr, wc = NA_WIN_ROWS, NA_WIN_COLS
    rpb = rpb.astype(F32)
    a = jnp.stack([rpb[:, wr - 1 - di:2 * wr - 1 - di, :] for di in range(wr)], axis=1)
    blocks = []
    for qs, nq, ks in NA_QBLOCKS:
        cols = []
        for q in range(qs, qs + nq):
            cstart = min(max(q - wc // 2, 0), GRID_W - wc)
            first = cstart - q + wc - 1
            off = cstart - ks
            cols.append(jnp.pad(a[..., first:first + wc],
                                ((0, 0), (0, 0), (0, 0), (off, NA_KEY_COLS - wc - off)),
                                constant_values=MASK_VALUE))
        t = jnp.stack(cols, axis=3)
        t = t.reshape(chunks, NA_HEADS_PER_CHUNK, wr, wr, nq, NA_KEY_COLS)
        t = t.transpose(0, 2, 1, 4, 3, 5)
        blocks.append(t.reshape(chunks, wr, NA_HEADS_PER_CHUNK * nq, wr * NA_KEY_COLS))
    return jnp.concatenate(blocks, axis=2)


def _na_kernel(q_ref, k_ref, v_ref, bias_ref, o_ref, l_ref, m_ref, p_ref, *, n_rows):
    rblk = pl.program_id(2)
    lane = lax.broadcasted_iota(jnp.int32, (1, 128), 1)
    hmask = [(lane // NA_HEAD_DIM == hp).astype(F32) for hp in range(NA_HEADS_PER_CHUNK)]
    win_tokens = NA_WIN_ROWS * GRID_W
    nkeys = NA_WIN_ROWS * NA_KEY_COLS
    ones = jnp.ones((nkeys, 128), BF16)

    def window(rr):
        r = rblk * NA_ROWS_PER_STEP + rr
        rs = jnp.clip(r - NA_WIN_ROWS // 2, 0, n_rows - NA_WIN_ROWS)
        return pl.multiple_of(rs * GRID_W, GRID_W), r - rs

    def key_block(win, ks):
        return jnp.concatenate([win[i * GRID_W + ks:i * GRID_W + ks + NA_KEY_COLS]
                                for i in range(NA_WIN_ROWS)], axis=0)

    def scores_and_max(grp):
        for j in range(NA_GROUP_ROWS):
            rr = grp * NA_GROUP_ROWS + j
            w0, di = window(rr)
            q0 = pl.multiple_of(rr * GRID_W, GRID_W)
            qrow = q_ref[pl.ds(q0, GRID_W), :].astype(F32)
            kwin = k_ref[pl.ds(w0, win_tokens), :]
            row0 = 0
            for qs, nq, ks in NA_QBLOCKS:
                nst = NA_HEADS_PER_CHUNK * nq
                qb = qrow[qs:qs + nq]
                qstack = jnp.concatenate([qb * hmask[hp] for hp in range(NA_HEADS_PER_CHUNK)],
                                         axis=0).astype(BF16)
                logits = lax.dot_general(qstack, key_block(kwin, ks), (((1,), (1,)), ((), ())),
                                         preferred_element_type=F32)
                l_ref[rr, pl.ds(row0, nst), :] = logits + bias_ref[di, pl.ds(row0, nst), :]
                row0 += nst
        for j in range(NA_GROUP_ROWS):
            rr = grp * NA_GROUP_ROWS + j
            m = jnp.max(l_ref[rr], axis=-1, keepdims=True)
            m_ref[rr] = jnp.broadcast_to(m, m_ref.shape[1:])

    def softmax_and_values(grp):
        for j in range(NA_GROUP_ROWS):
            rr = grp * NA_GROUP_ROWS + j
            m = m_ref[rr]
            p_ref[rr] = jnp.exp(l_ref[rr] - jnp.concatenate([m, m], axis=1)).astype(BF16)
        for j in range(NA_GROUP_ROWS):
            rr = grp * NA_GROUP_ROWS + j
            w0, _ = window(rr)
            vwin = v_ref[pl.ds(w0, win_tokens), :]
            outs = []
            row0 = 0
            for qs, nq, ks in NA_QBLOCKS:
                nst = NA_HEADS_PER_CHUNK * nq
                v1 = jnp.concatenate([key_block(vwin, ks), ones], axis=1)
                acc = jnp.dot(p_ref[rr, pl.ds(row0, nst), :], v1, preferred_element_type=F32)
                num = acc[0:nq, :128] * hmask[0]
                den = acc[0:nq, 128:] * hmask[0]
                for hp in range(1, NA_HEADS_PER_CHUNK):
                    num = num + acc[hp * nq:(hp + 1) * nq, :128] * hmask[hp]
                    den = den + acc[hp * nq:(hp + 1) * nq, 128:] * hmask[hp]
                outs.append(num * (1.0 / den))
                row0 += nst
            orow = jnp.concatenate(outs, axis=0)
            o_ref[pl.ds(pl.multiple_of(rr * GRID_W, GRID_W), GRID_W), :] = orow.astype(o_ref.dtype)

    n_groups = NA_ROWS_PER_STEP // NA_GROUP_ROWS
    scores_and_max(0)

    def body(g, carry):
        scores_and_max(g + 1)
        softmax_and_values(g)
        return carry

    lax.fori_loop(0, n_groups - 1, body, 0)
    softmax_and_values(n_groups - 1)


def _na_attention(qkv, bias_tab, batch, seq, d):
    n = qkv.shape[0]
    n_rows = seq // GRID_W
    chunks = d // 128
    step_tokens = NA_ROWS_PER_STEP * GRID_W
    rsteps = n_rows // NA_ROWS_PER_STEP
    nrow = bias_tab.shape[2]
    nkey = bias_tab.shape[3]
    return pl.pallas_call(
        functools.partial(_na_kernel, n_rows=n_rows),
        grid=(batch, chunks, rsteps),
        in_specs=[
            pl.BlockSpec((step_tokens, 128), lambda b, c, r: (b * rsteps + r, c)),
            pl.BlockSpec((seq, 128), lambda b, c, r: (b, chunks + c)),
            pl.BlockSpec((seq, 128), lambda b, c, r: (b, 2 * chunks + c)),
            pl.BlockSpec((None, NA_WIN_ROWS, nrow, nkey), lambda b, c, r: (c, 0, 0, 0)),
        ],
        out_specs=pl.BlockSpec((step_tokens, 128), lambda b, c, r: (b * rsteps + r, c)),
        out_shape=jax.ShapeDtypeStruct((n, d), BF16),
        scratch_shapes=[
            pltpu.VMEM((NA_ROWS_PER_STEP, nrow, nkey), F32),
            pltpu.VMEM((NA_ROWS_PER_STEP, nrow, 128), F32),
            pltpu.VMEM((NA_ROWS_PER_STEP, nrow, nkey), BF16),
        ],
        compiler_params=_cparams(("arbitrary", "arbitrary", "arbitrary")),
        name="na_attention",
    )(qkv, qkv, qkv, bias_tab)


def _log_sigmoid(v):
    return jnp.minimum(v, 0.0) - jnp.log(1.0 + jnp.exp(-jnp.abs(v)))


def _gla_kernel(q_ref, k_ref, v_ref, r_ref, a_ref, wg_ref, bg_ref, gn_ref, o_ref,
                st_ref, of_ref, cum_ref, qe_ref, ke_ref, kdt_ref, att_ref, av_ref, u_ref, dec_ref, os_ref,
                *, cb, nb):
    p = pl.program_id(2)
    i = pl.program_id(3)
    fwd = p == 0
    step_rows = cb * GLA_CHUNK

    @pl.when(i == 0)
    def _():
        st_ref[...] = jnp.zeros_like(st_ref)

    blk = jnp.where(fwd, i, nb - 1 - i)
    tr = lax.broadcasted_iota(jnp.int32, (GLA_CHUNK, GLA_CHUNK), 0)
    tc = lax.broadcasted_iota(jnp.int32, (GLA_CHUNK, GLA_CHUNK), 1)
    tmask = (tr - tc) * (1 - 2 * p) >= 0
    tmat = tmask.astype(BF16)
    nt = (((1,), (1,)), ((), ()))

    gl = jnp.dot(a_ref[...], wg_ref[...], preferred_element_type=F32) + bg_ref[...]
    g = _log_sigmoid(gl) * (1.0 / GLA_GATE_NORM)
    g1 = g.astype(BF16)
    e1 = g - g1.astype(F32)
    g2 = e1.astype(BF16)
    g3 = (e1 - g2.astype(F32)).astype(BF16)
    for c in range(cb):
        rows = slice(c * GLA_CHUNK, (c + 1) * GLA_CHUNK)
        cum_ref[rows, :] = (jnp.dot(tmat, g1[rows], preferred_element_type=F32)
                            + jnp.dot(tmat, g2[rows], preferred_element_type=F32)
                            + jnp.dot(tmat, g3[rows], preferred_element_type=F32))
    for c in range(cb):
        rows = slice(c * GLA_CHUNK, (c + 1) * GLA_CHUNK)
        cum = cum_ref[rows, :]
        last = jnp.where(fwd, cum[GLA_CHUNK - 1:GLA_CHUNK], cum[0:1])
        q = q_ref[rows, :].astype(F32)
        k = k_ref[rows, :].astype(F32)
        qe_ref[c] = (q * jnp.exp(cum)).astype(BF16)
        ke_ref[c] = (k * jnp.exp(-cum)).astype(BF16)
        kdt_ref[c] = (k * jnp.exp(last - cum)).T.astype(BF16)
        dcol = jnp.broadcast_to(jnp.exp(last), (8, last.shape[1])).T
        dec_ref[c] = jnp.broadcast_to(dcol[:, 0:1], dec_ref.shape[1:])
    for c in range(cb):
        att = lax.dot_general(qe_ref[c], ke_ref[c], nt, preferred_element_type=F32)
        att_ref[c] = jnp.where(tmask, att, 0.0).astype(BF16)
    for c in range(cb):
        rows = slice(c * GLA_CHUNK, (c + 1) * GLA_CHUNK)
        v = v_ref[rows, :]
        av_ref[c] = jnp.dot(att_ref[c], v, preferred_element_type=F32)
        u_ref[c] = jnp.dot(kdt_ref[c], v, preferred_element_type=F32)

    lane_tiles = st_ref.shape[1] // 128
    for cc in range(cb):
        c = jnp.where(fwd, cc, cb - 1 - cc)
        st = st_ref[...]
        o = av_ref[c] + jnp.dot(qe_ref[c], st.astype(BF16), preferred_element_type=F32)
        dec = dec_ref[c]
        st_ref[...] = st * jnp.concatenate([dec] * lane_tiles, axis=1) + u_ref[c]
        os_ref[pl.ds(pl.multiple_of(c * GLA_CHUNK, GLA_CHUNK), GLA_CHUNK), :] = o

    g0 = pl.multiple_of(blk * step_rows, step_rows)

    @pl.when(fwd)
    def _():
        of_ref[pl.ds(g0, step_rows), :] = os_ref[...]

    @pl.when(jnp.logical_not(fwd))
    def _():
        ot = of_ref[pl.ds(g0, step_rows), :] + os_ref[...]
        ms = jnp.mean(ot * ot, axis=-1, keepdims=True)
        y = ot * lax.rsqrt(ms + EPS) * gn_ref[...]
        y = y * _silu(r_ref[...].astype(F32))
        o_ref[...] = y.astype(o_ref.dtype)


def _gla(proj, a_low, wg_pad, b_gate, gn_g, batch, seq, d):
    n = proj.shape[0]
    hk = d // 2 // GLA_HEADS
    hv = d // GLA_HEADS
    cb = min(GLA_CHUNKS_PER_STEP, seq // GLA_CHUNK)
    step_rows = cb * GLA_CHUNK
    nb = seq // step_rows

    def rowblk(b, p, i):
        return b * nb + jnp.where(p == 0, i, nb - 1 - i)

    def outblk(b, p, i):
        return b * nb + jnp.where(p == 0, nb - 1, nb - 1 - i)

    return pl.pallas_call(
        functools.partial(_gla_kernel, cb=cb, nb=nb),
        grid=(batch, GLA_HEADS, 2, nb),
        in_specs=[
            pl.BlockSpec((step_rows, hk), lambda b, h, p, i: (rowblk(b, p, i), h)),
            pl.BlockSpec((step_rows, hk), lambda b, h, p, i: (rowblk(b, p, i), GLA_HEADS + h)),
            pl.BlockSpec((step_rows, hv), lambda b, h, p, i: (rowblk(b, p, i), GLA_HEADS + h)),
            pl.BlockSpec((step_rows, hv), lambda b, h, p, i: (rowblk(b, p, i), 2 * GLA_HEADS + h)),
            pl.BlockSpec((step_rows, 128), lambda b, h, p, i: (rowblk(b, p, i), 0)),
            pl.BlockSpec((None, 128, hk), lambda b, h, p, i: (p, 0, h)),
            pl.BlockSpec((None, 1, hk), lambda b, h, p, i: (p, 0, h)),
            pl.BlockSpec((1, hv), lambda b, h, p, i: (0, h)),
        ],
        out_specs=pl.BlockSpec((step_rows, hv), lambda b, h, p, i: (outblk(b, p, i), h)),
        out_shape=jax.ShapeDtypeStruct((n, d), BF16),
        scratch_shapes=[
            pltpu.VMEM((hk, hv), F32),
            pltpu.VMEM((seq, hv), F32),
            pltpu.VMEM((step_rows, hk), F32),
            pltpu.VMEM((cb, GLA_CHUNK, hk), BF16),
            pltpu.VMEM((cb, GLA_CHUNK, hk), BF16),
            pltpu.VMEM((cb, hk, GLA_CHUNK), BF16),
            pltpu.VMEM((cb, GLA_CHUNK, GLA_CHUNK), BF16),
            pltpu.VMEM((cb, GLA_CHUNK, hv), F32),
            pltpu.VMEM((cb, hk, hv), F32),
            pltpu.VMEM((cb, hk, 128), F32),
            pltpu.VMEM((step_rows, hv), F32),
        ],
        compiler_params=_cparams(("arbitrary", "arbitrary", "arbitrary", "arbitrary")),
        name="gla",
    )(proj, proj, proj, proj, a_low, wg_pad, b_gate.reshape(2, 1, -1), gn_g.reshape(1, d))


def _router_kernel(x_ref, g_ref, sc_ref, sh_ref, w_ref, h_ref, eid_ref, rt_ref, cnt_ref, run_ref):
    @pl.when(pl.program_id(0) == 0)
    def _():
        run_ref[...] = jnp.zeros_like(run_ref)

    h = _norm_mod(x_ref[...], g_ref[...], sc_ref[...], sh_ref[...])
    hb = h.astype(BF16)
    half = h.shape[1] // 2
    bits = lax.bitcast_convert_type(hb.astype(F32), jnp.uint32)
    h_ref[...] = (bits[:, :half] >> 16) | (bits[:, half:] & jnp.uint32(0xFFFF0000))
    logits = jnp.dot(hb, w_ref[...], preferred_element_type=F32)
    lane = lax.broadcasted_iota(jnp.int32, logits.shape, 1)
    gl = jnp.where(lane < MOE_GROUPS, logits, MASK_VALUE)
    gmax = jnp.max(gl, axis=-1, keepdims=True)
    gidx = jnp.min(jnp.where(gl == gmax, lane, ROUTER_LANES), axis=-1, keepdims=True)
    g_w = 1.0 / jnp.sum(jnp.exp(gl - gmax), axis=-1, keepdims=True)
    e_lane = lane - MOE_GROUPS
    in_group = (e_lane >= 0) & (e_lane < MOE_EXPERTS) & ((e_lane >> 3) == gidx)
    el = jnp.where(in_group, logits, MASK_VALUE)
    m1 = jnp.max(el, axis=-1, keepdims=True)
    i1 = jnp.min(jnp.where(el == m1, lane, ROUTER_LANES), axis=-1, keepdims=True)
    el2 = jnp.where(lane == i1, MASK_VALUE, el)
    m2 = jnp.max(el2, axis=-1, keepdims=True)
    i2 = jnp.min(jnp.where(el2 == m2, lane, ROUTER_LANES), axis=-1, keepdims=True)
    t = jnp.exp(m2 - m1)
    w1 = g_w / (1.0 + t)
    w2 = w1 * t
    rt_ref[...] = jnp.where(lane == 0, w1, jnp.where(lane == 1, w2, 0.0))
    tm = logits.shape[0]
    oh1 = lane == i1
    oh2 = lane == i2
    both = (oh1 | oh2).astype(BF16)
    tr = lax.broadcasted_iota(jnp.int32, (tm, tm), 0)
    tc = lax.broadcasted_iota(jnp.int32, (tm, tm), 1)
    before = (tc < tr).astype(BF16)
    prior = jnp.dot(before, both, preferred_element_type=F32) + run_ref[0:1, :]
    r1 = jnp.sum(jnp.where(oh1, prior, 0.0), axis=-1, keepdims=True).astype(jnp.int32)
    r2 = jnp.sum(jnp.where(oh2, prior, 0.0), axis=-1, keepdims=True).astype(jnp.int32)
    eid_ref[...] = jnp.where(lane == 0, i1 - MOE_GROUPS,
                             jnp.where(lane == 1, i2 - MOE_GROUPS,
                                       jnp.where(lane == 2, r1, jnp.where(lane == 3, r2, 0))))
    run = run_ref[...] + jnp.sum(both.astype(F32), axis=0, keepdims=True)
    run_ref[...] = run
    cnt_ref[...] = run


def _router(x, g, sc, sh, w_router, seq):
    n, d = x.shape
    tm = min(512, seq)
    return pl.pallas_call(
        _router_kernel,
        grid=(n // tm,),
        in_specs=[
            pl.BlockSpec((tm, d), lambda i: (i, 0)),
            pl.BlockSpec((1, d), lambda i: (0, 0)),
            pl.BlockSpec((None, 1, d), lambda i: (i * tm // seq, 0, 0)),
            pl.BlockSpec((None, 1, d), lambda i: (i * tm // seq, 0, 0)),
            pl.BlockSpec((d, ROUTER_LANES), lambda i: (0, 0)),
        ],
        out_specs=[
            pl.BlockSpec((tm, d // 2), lambda i: (i, 0)),
            pl.BlockSpec((tm, ROUTER_LANES), lambda i: (i, 0)),
            pl.BlockSpec((tm, ROUTER_LANES), lambda i: (i, 0)),
            pl.BlockSpec((8, ROUTER_LANES), lambda i: (0, 0)),
        ],
        out_shape=[
            jax.ShapeDtypeStruct((n, d // 2), jnp.uint32),
            jax.ShapeDtypeStruct((n, ROUTER_LANES), jnp.int32),
            jax.ShapeDtypeStruct((n, ROUTER_LANES), F32),
            jax.ShapeDtypeStruct((8, ROUTER_LANES), F32),
        ],
        scratch_shapes=[pltpu.VMEM((8, ROUTER_LANES), F32)],
        compiler_params=_cparams(("arbitrary",)),
        name="moe_router",
    )(x, g.reshape(1, d), sc, sh, w_router)


def _dispatch_tables(eid, cnt):
    n = eid.shape[0]
    a = n * 2
    flat_e = eid[:, 0:2].reshape(a)
    rank = eid[:, 2:4].reshape(a)
    counts = cnt[0, MOE_GROUPS:MOE_GROUPS + MOE_EXPERTS].astype(jnp.int32)
    padded = (counts + MOE_BLOCK - 1) // MOE_BLOCK * MOE_BLOCK
    pend = jnp.cumsum(padded)
    pstart = pend - padded
    onehot = flat_e[:, None] == jnp.arange(MOE_EXPERTS, dtype=jnp.int32)[None, :]
    dest = jnp.sum(jnp.where(onehot, pstart[None, :], 0), axis=1) + rank
    n_blocks = (a + MOE_EXPERTS * MOE_BLOCK) // MOE_BLOCK
    bstart = jnp.arange(n_blocks, dtype=jnp.int32) * MOE_BLOCK
    block_expert = jnp.minimum(jnp.sum((pend[None, :] <= bstart[:, None]).astype(jnp.int32), axis=1),
                               MOE_EXPERTS - 1)
    nvalid = (pend[-1] // MOE_BLOCK).astype(jnp.int32).reshape(1)
    dest2 = dest.reshape(n, 2)
    return block_expert, nvalid, dest2[:, 0], dest2[:, 1], pstart + counts, pend


def _moe_kernel(be_ref, d0_ref, d1_ref, nv_ref, plo_ref, phi_ref, h_hbm, wup_ref, wdn_ref, o_ref,
                tok_ref, xbuf, xb_ref, wup_bf, wdn_bf, sem, *, tb, ff):
    b = pl.program_id(0)
    nvalid = nv_ref[0]
    n_tok = d0_ref.shape[0]

    def row_copy(tok, i, slot):
        return pltpu.make_async_copy(h_hbm.at[pl.ds(tok, 1), :], xbuf.at[slot, pl.ds(i, 1), :],
                                     sem.at[slot])

    def wait_rows(slot):
        pltpu.make_async_copy(h_hbm.at[pl.ds(0, tb), :], xbuf.at[slot], sem.at[slot]).wait()

    @pl.when(b == 0)
    def _():
        def fill(i, carry):
            for u in range(MOE_FILL_UNROLL):
                t = i * MOE_FILL_UNROLL + u
                tok_ref[d0_ref[t]] = t
                tok_ref[d1_ref[t]] = t
            return carry
        lax.fori_loop(0, n_tok // MOE_FILL_UNROLL, fill, 0)

        def pad_expert(e, carry):
            def pad(s, c):
                tok_ref[s] = 0
                return c
            lax.fori_loop(plo_ref[e], phi_ref[e], pad, 0)
            return carry
        lax.fori_loop(0, MOE_EXPERTS, pad_expert, 0)

        for ahead in range(MOE_GATHER_DEPTH - 1):
            base = jnp.minimum(ahead, nvalid - 1) * tb

            def body(i, carry, base=base, ahead=ahead):
                row_copy(tok_ref[base + i], i, ahead).start()
                return carry
            lax.fori_loop(0, tb, body, 0)

    prev = be_ref[jnp.maximum(b - 1, 0)]

    @pl.when((b < nvalid) & ((b == 0) | (be_ref[b] != prev)))
    def _():
        wup_bf[...] = wup_ref[...].astype(BF16)
        wdn_bf[...] = wdn_ref[...].astype(BF16)

    @pl.when(b < nvalid)
    def _():
        slot = lax.rem(b, MOE_GATHER_DEPTH)
        wait_rows(slot)
        xp = xbuf[slot]
        half = xp.shape[1]
        xb_ref[:, :half] = lax.bitcast_convert_type(xp << 16, F32).astype(BF16)
        xb_ref[:, half:] = lax.bitcast_convert_type(xp & jnp.uint32(0xFFFF0000), F32).astype(BF16)
        ahead = MOE_GATHER_DEPTH - 1
        nslot = lax.rem(b + ahead, MOE_GATHER_DEPTH)
        base = jnp.minimum(b + ahead, nvalid - 1) * tb
        for i in range(tb):
            row_copy(tok_ref[base + i], i, nslot).start()
        hcat = jnp.dot(xb_ref[...], wup_bf[...], preferred_element_type=F32)
        act = (_silu(hcat[:, :ff]) * hcat[:, ff:]).astype(BF16)
        o_ref[...] = jnp.dot(act, wdn_bf[...], preferred_element_type=F32)

        @pl.when(b == nvalid - 1)
        def _():
            for k in range(1, MOE_GATHER_DEPTH):
                wait_rows(lax.rem(b + k, MOE_GATHER_DEPTH))

    @pl.when(b >= nvalid)
    def _():
        o_ref[...] = jnp.zeros_like(o_ref)


def _moe_experts(h2, block_expert, d0, d1, nvalid, pad_lo, pad_hi, w_up, w_down, layer):
    n = h2.shape[0]
    d = 2 * h2.shape[1]
    tb = MOE_BLOCK
    n_blocks = block_expert.shape[0]
    n_slots = n_blocks * tb
    ff = w_down.shape[2]
    grid_spec = pltpu.PrefetchScalarGridSpec(
        num_scalar_prefetch=6,
        grid=(n_blocks,),
        in_specs=[
            pl.BlockSpec(memory_space=pl.ANY),
            pl.BlockSpec((None, None, d, 2 * ff), lambda b, be, *_: (layer, be[b], 0, 0)),
            pl.BlockSpec((None, None, ff, d), lambda b, be, *_: (layer, be[b], 0, 0)),
        ],
        out_specs=pl.BlockSpec((tb, d), lambda b, be, *_: (b, 0)),
        scratch_shapes=[
            pltpu.SMEM((n_slots,), jnp.int32),
            pltpu.VMEM((MOE_GATHER_DEPTH, tb, d // 2), jnp.uint32),
            pltpu.VMEM((tb, d), BF16),
            pltpu.VMEM((d, 2 * ff), BF16),
            pltpu.VMEM((ff, d), BF16),
            pltpu.SemaphoreType.DMA((MOE_GATHER_DEPTH,)),
        ],
    )
    return pl.pallas_call(
        functools.partial(_moe_kernel, tb=tb, ff=ff),
        grid_spec=grid_spec,
        out_shape=jax.ShapeDtypeStruct((n_slots, d), F32),
        compiler_params=_cparams(("arbitrary",)),
        name="moe_experts",
    )(block_expert, d0, d1, nvalid, pad_lo, pad_hi, h2, w_up, w_down)


def _combine_kernel(d0_ref, d1_ref, yb_hbm, x_ref, rt_ref, gate_ref, *rest, tm, tail):
    if tail == "codes":
        g_ref, sc_ref, sh_ref, wa_ref, o_ref, a_ref, ybuf, sem = rest
    elif tail == "final":
        g_ref, o_ref, ybuf, sem = rest
    else:
        o_ref, ybuf, sem = rest
    t = pl.program_id(0)
    nt = pl.num_programs(0)

    def row_copy(src_row, i, slot):
        return pltpu.make_async_copy(yb_hbm.at[pl.ds(src_row, 1), :], ybuf.at[slot, pl.ds(i, 1), :],
                                     sem.at[slot])

    def wait_rows(slot):
        pltpu.make_async_copy(yb_hbm.at[pl.ds(0, 2 * tm), :], ybuf.at[slot], sem.at[slot]).wait()

    def gather(blk, slot):
        base = blk * tm
        for i in range(tm):
            row_copy(d0_ref[base + i], i, slot).start()
            row_copy(d1_ref[base + i], tm + i, slot).start()

    ahead = MOE_GATHER_DEPTH - 1

    @pl.when(t == 0)
    def _():
        for k in range(ahead):
            gather(jnp.minimum(k, nt - 1), k)

    slot = lax.rem(t, MOE_GATHER_DEPTH)
    wait_rows(slot)
    gather(jnp.minimum(t + ahead, nt - 1), lax.rem(t + ahead, MOE_GATHER_DEPTH))
    rt = rt_ref[...]
    y = rt[:, 0:1] * ybuf[slot, pl.ds(0, tm), :] + rt[:, 1:2] * ybuf[slot, pl.ds(tm, tm), :]
    xn = x_ref[...] + gate_ref[...] * y
    if tail == "final":
        ms = jnp.mean(xn * xn, axis=-1, keepdims=True)
        o_ref[...] = xn * lax.rsqrt(ms + EPS) * g_ref[...]
    else:
        o_ref[...] = xn
    if tail == "codes":
        h = _norm_mod(xn, g_ref[...], sc_ref[...], sh_ref[...]).astype(BF16)
        a_ref[...] = jnp.dot(h, wa_ref[...], preferred_element_type=F32).astype(a_ref.dtype)

    @pl.when(t == nt - 1)
    def _():
        for k in range(1, MOE_GATHER_DEPTH):
            wait_rows(lax.rem(t + k, MOE_GATHER_DEPTH))


def _combine(yb, d0, d1, x, route, gate, seq, tail="plain", norm=None):
    n, d = x.shape
    tm = min(256, seq)
    row = lambda t, d0, d1: (t, 0)
    const = lambda t, d0, d1: (0, 0)
    per_batch = lambda t, d0, d1: (t * tm // seq, 0, 0)
    in_specs = [
        pl.BlockSpec(memory_space=pl.ANY),
        pl.BlockSpec((tm, d), row),
        pl.BlockSpec((tm, ROUTER_LANES), row),
        pl.BlockSpec((None, 1, d), per_batch),
    ]
    args = [yb, x, route, gate]
    out_specs = pl.BlockSpec((tm, d), row)
    out_shape = jax.ShapeDtypeStruct((n, d), F32)
    if tail == "codes":
        g, sc, sh, w_a = norm
        in_specs += [pl.BlockSpec((1, d), const), pl.BlockSpec((None, 1, d), per_batch),
                     pl.BlockSpec((None, 1, d), per_batch), pl.BlockSpec((d, 128), const)]
        args += [g.reshape(1, d), sc, sh, w_a]
        out_specs = [out_specs, pl.BlockSpec((tm, 128), row)]
        out_shape = [out_shape, jax.ShapeDtypeStruct((n, 128), BF16)]
    elif tail == "final":
        in_specs += [pl.BlockSpec((1, d), const)]
        args += [norm[0].reshape(1, d)]
    grid_spec = pltpu.PrefetchScalarGridSpec(
        num_scalar_prefetch=2,
        grid=(n // tm,),
        in_specs=in_specs,
        out_specs=out_specs,
        scratch_shapes=[pltpu.VMEM((MOE_GATHER_DEPTH, 2 * tm, d), F32),
                        pltpu.SemaphoreType.DMA((MOE_GATHER_DEPTH,))],
    )
    return pl.pallas_call(
        functools.partial(_combine_kernel, tm=tm, tail=tail),
        grid_spec=grid_spec,
        out_shape=out_shape,
        compiler_params=_cparams(("arbitrary",)),
        name="moe_combine",
    )(d0, d1, *args)


def _proj_tile(nout):
    for tn in (1536, 896, 768, 512, 256, 128):
        if nout % tn == 0:
            return tn
    raise ValueError(f"no projection tile for width {nout}")


def kernel(x, c, ada_w, ada_b, norm1_g, norm2_g, na_w_in, na_w_out, na_rpb, gla_w_in, gla_w_gate_up, gla_b_gate, gla_gn_g, gla_w_out, moe_w_router_group, moe_w_router_expert, moe_w_up, moe_w_down, final_g):
    batch, seq, d = x.shape
    depth = ada_w.shape[0]
    n = batch * seq
    assert seq % (GRID_W * NA_ROWS_PER_STEP) == 0 and d % 1024 == 0
    dk = d // 2

    mod = _adaln(c, ada_w, ada_b)
    xf = x.reshape(n, d)
    i_na = 0
    i_gla = 0
    for i in range(depth):
        sh1, sc1, gt1, sh2, sc2, gt2 = [m.reshape(batch, 1, d) for m in jnp.split(mod[i], 6, axis=-1)]
        if i % 2 == 0:
            w_in = na_w_in[i_na].astype(BF16)
            colscale = jnp.concatenate([jnp.full((d,), NA_HEAD_DIM ** -0.5, F32), jnp.ones((2 * d,), F32)])
            qkv = _norm_mod_matmul(xf, norm1_g[i], sc1, sh1, w_in, colscale, seq, _proj_tile(3 * d))
            bias_tab = _na_bias_table(na_rpb[i_na])
            y = _na_attention(qkv, bias_tab, batch, seq, d)
            w_out = na_w_out[i_na].astype(BF16)
            i_na += 1
        else:
            hk = dk // GLA_HEADS
            w_in = gla_w_in[i_gla][:, :3 * d].astype(BF16)
            colscale = jnp.concatenate([jnp.full((dk,), hk ** -0.5, F32), jnp.ones((3 * d - dk,), F32)])
            proj = _norm_mod_matmul(xf, norm1_g[i], sc1, sh1, w_in, colscale, seq, _proj_tile(3 * d))
            wg_pad = jnp.zeros((2, 128, dk), F32)
            wg_pad = wg_pad.at[0, :GLA_GATE_RANK].set(gla_w_gate_up[i_gla, 0])
            wg_pad = wg_pad.at[1, GLA_GATE_RANK:2 * GLA_GATE_RANK].set(gla_w_gate_up[i_gla, 1])
            y = _gla(proj, a_low, wg_pad.astype(BF16), gla_b_gate[i_gla], gla_gn_g[i_gla], batch, seq, d)
            w_out = gla_w_out[i_gla].astype(BF16)
            i_gla += 1
        xf = _matmul_residual(y, w_out, xf, gt1, seq)

        w_router = jnp.concatenate(
            [moe_w_router_group[i], moe_w_router_expert[i],
             jnp.zeros((d, ROUTER_LANES - MOE_GROUPS - MOE_EXPERTS), F32)], axis=1).astype(BF16)
        h2, eid, route, cnt = _router(xf, norm2_g[i], sc2, sh2, w_router, seq)
        block_expert, nvalid, d0, d1, pad_lo, pad_hi = _dispatch_tables(eid, cnt)
        yb = _moe_experts(h2, block_expert, d0, d1, nvalid, pad_lo, pad_hi, moe_w_up, moe_w_down, i)
        if i == depth - 1:
            xf = _combine(yb, d0, d1, xf, route, gt2, seq, "final", (final_g,))
        elif (i + 1) % 2 == 1:
            nsh1, nsc1 = [m.reshape(batch, 1, d) for m in jnp.split(mod[i + 1], 6, axis=-1)[:2]]
            w_a = jnp.concatenate([gla_w_in[i_gla][:, 3 * d:],
                                   jnp.zeros((d, 128 - 2 * GLA_GATE_RANK), F32)], axis=1).astype(BF16)
            xf, a_low = _combine(yb, d0, d1, xf, route, gt2, seq, "codes",
                                 (norm1_g[i + 1], nsc1, nsh1, w_a))
        else:
            xf = _combine(yb, d0, d1, xf, route, gt2, seq)

    return xf.reshape(batch, seq, d)
```

```python
import functools

import jax
import jax.numpy as jnp
import numpy as np
from jax import lax
from jax.experimental import pallas as pl
from jax.experimental.pallas import tpu as pltpu

F32 = jnp.float32
BF16 = jnp.bfloat16

EPS = 1e-6
GRID_W = 64
NA_HEAD_DIM = 32
NA_WIN_ROWS = 8
NA_WIN_COLS = 16
NA_HEADS_PER_CHUNK = 4
NA_QBLOCKS = ((0, 24, 0), (24, 16, 16), (40, 24, 32))
NA_KEY_COLS = 32
NA_ROWS_PER_STEP = 32
NA_GROUP_ROWS = 32
MASK_VALUE = -1e30

GLA_HEADS = 4
GLA_GATE_RANK = 16
GLA_GATE_NORM = 16.0
GLA_CHUNK = 64
GLA_CHUNKS_PER_STEP = 16

MOE_GROUPS = 4
MOE_EXPERTS_PER_GROUP = 8
MOE_EXPERTS = MOE_GROUPS * MOE_EXPERTS_PER_GROUP
MOE_BLOCK = 256
MOE_GATHER_DEPTH = 3
MOE_FILL_UNROLL = 8
ROUTER_LANES = 128

VMEM_LIMIT = 56 * 1024 * 1024


def _cparams(sem):
    return pltpu.CompilerParams(dimension_semantics=sem, vmem_limit_bytes=VMEM_LIMIT)


def _silu(v):
    return v * jax.nn.sigmoid(v)


def _adaln_kernel(c_ref, w_ref, b_ref, o_ref):
    cond = _silu(c_ref[...]).astype(BF16)
    w = w_ref[...].astype(BF16)
    o_ref[...] = jnp.dot(cond, w, preferred_element_type=F32) + b_ref[...]


def _adaln(c, ada_w, ada_b):
    depth, d, n6 = ada_w.shape
    b = c.shape[0]
    rows = 16
    cpad = jnp.zeros((rows, d), F32).at[:b].set(c)
    tn = 1024
    out = pl.pallas_call(
        _adaln_kernel,
        grid=(depth, n6 // tn),
        in_specs=[
            pl.BlockSpec((rows, d), lambda i, j: (0, 0)),
            pl.BlockSpec((None, d, tn), lambda i, j: (i, 0, j)),
            pl.BlockSpec((None, 1, tn), lambda i, j: (i, 0, j)),
        ],
        out_specs=pl.BlockSpec((None, rows, tn), lambda i, j: (i, 0, j)),
        out_shape=jax.ShapeDtypeStruct((depth, rows, n6), F32),
        compiler_params=_cparams(("arbitrary", "arbitrary")),
        name="adaln",
    )(cpad, ada_w, ada_b.reshape(depth, 1, n6))
    return out[:, :b]


def _norm_mod(x, g, sc, sh):
    ms = jnp.mean(x * x, axis=-1, keepdims=True)
    y = x * lax.rsqrt(ms + EPS) * g
    return y * (1.0 + sc) + sh


def _nmm_kernel(x_ref, g_ref, sc_ref, sh_ref, w_ref, cs_ref, o_ref, h_ref):
    @pl.when(pl.program_id(1) == 0)
    def _():
        h_ref[...] = _norm_mod(x_ref[...], g_ref[...], sc_ref[...], sh_ref[...]).astype(BF16)

    acc = jnp.dot(h_ref[...], w_ref[...], preferred_element_type=F32)
    o_ref[...] = (acc * cs_ref[...]).astype(o_ref.dtype)


def _norm_mod_matmul(x, g, sc, sh, w, colscale, seq, tn):
    n, d = x.shape
    nout = w.shape[1]
    tm = min(1024, seq)
    return pl.pallas_call(
        _nmm_kernel,
        grid=(n // tm, nout // tn),
        in_specs=[
            pl.BlockSpec((tm, d), lambda i, j: (i, 0)),
            pl.BlockSpec((1, d), lambda i, j: (0, 0)),
            pl.BlockSpec((None, 1, d), lambda i, j: (i * tm // seq, 0, 0)),
            pl.BlockSpec((None, 1, d), lambda i, j: (i * tm // seq, 0, 0)),
            pl.BlockSpec((d, tn), lambda i, j: (0, j)),
            pl.BlockSpec((1, tn), lambda i, j: (0, j)),
        ],
        out_specs=pl.BlockSpec((tm, tn), lambda i, j: (i, j)),
        out_shape=jax.ShapeDtypeStruct((n, nout), BF16),
        scratch_shapes=[pltpu.VMEM((tm, d), BF16)],
        compiler_params=_cparams(("arbitrary", "arbitrary")),
        name="norm_mod_matmul",
    )(x, g.reshape(1, d), sc, sh, w, colscale.reshape(1, nout))


def _mmres_kernel(a_ref, w_ref, res_ref, gate_ref, o_ref):
    acc = jnp.dot(a_ref[...], w_ref[...], preferred_element_type=F32)
    o_ref[...] = res_ref[...] + gate_ref[...] * acc


def _matmul_residual(a, w, res, gate, seq):
    n, k = a.shape
    d = w.shape[1]
    tm = min(512, seq)
    return pl.pallas_call(
        _mmres_kernel,
        grid=(n // tm,),
        in_specs=[
            pl.BlockSpec((tm, k), lambda i: (i, 0)),
            pl.BlockSpec((k, d), lambda i: (0, 0)),
            pl.BlockSpec((tm, d), lambda i: (i, 0)),
            pl.BlockSpec((None, 1, d), lambda i: (i * tm // seq, 0, 0)),
        ],
        out_specs=pl.BlockSpec((tm, d), lambda i: (i, 0)),
        out_shape=jax.ShapeDtypeStruct((n, d), F32),
        compiler_params=_cparams(("arbitrary",)),
        name="matmul_residual",
    )(a, w, res, gate)


def _na_bias_table(rpb):
    heads = rpb.shape[0]
    chunks = heads // NA_HEADS_PER_CHUNK
    wr, wc = NA_WIN_ROWS, NA_WIN_COLS
    rpb = rpb.astype(F32)
    a = jnp.stack([rpb[:, wr - 1 - di:2 * wr - 1 - di, :] for di in range(wr)], axis=1)
    blocks = []
    for qs, nq, ks in NA_QBLOCKS:
        cols = []
        for q in range(qs, qs + nq):
            cstart = min(max(q - wc // 2, 0), GRID_W - wc)
            first = cstart - q + wc - 1
            off = cstart - ks
            cols.append(jnp.pad(a[..., first:first + wc],
                                ((0, 0), (0, 0), (0, 0), (off, NA_KEY_COLS - wc - off)),
                                constant_values=MASK_VALUE))
        t = jnp.stack(cols, axis=3)
        t = t.reshape(chunks, NA_HEADS_PER_CHUNK, wr, wr, nq, NA_KEY_COLS)
        t = t.transpose(0, 2, 1, 4, 3, 5)
        blocks.append(t.reshape(chunks, wr, NA_HEADS_PER_CHUNK * nq, wr * NA_KEY_COLS))
    return jnp.concatenate(blocks, axis=2)


def _na_kernel(q_ref, k_ref, v_ref, bias_ref, o_ref, l_ref, m_ref, p_ref, *, n_rows):
    rblk = pl.program_id(2)
    lane = lax.broadcasted_iota(jnp.int32, (1, 128), 1)
    hmask = [(lane // NA_HEAD_DIM == hp).astype(F32) for hp in range(NA_HEADS_PER_CHUNK)]
    win_tokens = NA_WIN_ROWS * GRID_W
    nkeys = NA_WIN_ROWS * NA_KEY_COLS
    ones = jnp.ones((nkeys, 128), BF16)

    def window(rr):
        r = rblk * NA_ROWS_PER_STEP + rr
        rs = jnp.clip(r - NA_WIN_ROWS // 2, 0, n_rows - NA_WIN_ROWS)
        return pl.multiple_of(rs * GRID_W, GRID_W), r - rs

    def key_block(win, ks):
        return jnp.concatenate([win[i * GRID_W + ks:i * GRID_W + ks + NA_KEY_COLS]
                                for i in range(NA_WIN_ROWS)], axis=0)

    def scores_and_max(grp):
        for j in range(NA_GROUP_ROWS):
            rr = grp * NA_GROUP_ROWS + j
            w0, di = window(rr)
            q0 = pl.multiple_of(rr * GRID_W, GRID_W)
            qrow = q_ref[pl.ds(q0, GRID_W), :].astype(F32)
            kwin = k_ref[pl.ds(w0, win_tokens), :]
            row0 = 0
            for qs, nq, ks in NA_QBLOCKS:
                nst = NA_HEADS_PER_CHUNK * nq
                qb = qrow[qs:qs + nq]
                qstack = jnp.concatenate([qb * hmask[hp] for hp in range(NA_HEADS_PER_CHUNK)],
                                         axis=0).astype(BF16)
                logits = lax.dot_general(qstack, key_block(kwin, ks), (((1,), (1,)), ((), ())),
                                         preferred_element_type=F32)
                l_ref[rr, pl.ds(row0, nst), :] = logits + bias_ref[di, pl.ds(row0, nst), :]
                row0 += nst
        for j in range(NA_GROUP_ROWS):
            rr = grp * NA_GROUP_ROWS + j
            m = jnp.max(l_ref[rr], axis=-1, keepdims=True)
            m_ref[rr] = jnp.broadcast_to(m, m_ref.shape[1:])

    def softmax_and_values(grp):
        for j in range(NA_GROUP_ROWS):
            rr = grp * NA_GROUP_ROWS + j
            m = m_ref[rr]
            p_ref[rr] = jnp.exp(l_ref[rr] - jnp.concatenate([m, m], axis=1)).astype(BF16)
        for j in range(NA_GROUP_ROWS):
            rr = grp * NA_GROUP_ROWS + j
            w0, _ = window(rr)
            vwin = v_ref[pl.ds(w0, win_tokens), :]
            outs = []
            row0 = 0
            for qs, nq, ks in NA_QBLOCKS:
                nst = NA_HEADS_PER_CHUNK * nq
                v1 = jnp.concatenate([key_block(vwin, ks), ones], axis=1)
                acc = jnp.dot(p_ref[rr, pl.ds(row0, nst), :], v1, preferred_element_type=F32)
                num = acc[0:nq, :128] * hmask[0]
                den = acc[0:nq, 128:] * hmask[0]
                for hp in range(1, NA_HEADS_PER_CHUNK):
                    num = num + acc[hp * nq:(hp + 1) * nq, :128] * hmask[hp]
                    den = den + acc[hp * nq:(hp + 1) * nq, 128:] * hmask[hp]
                outs.append(num * (1.0 / den))
                row0 += nst
            orow = jnp.concatenate(outs, axis=0)
            o_ref[pl.ds(pl.multiple_of(rr * GRID_W, GRID_W), GRID_W), :] = orow.astype(o_ref.dtype)

    n_groups = NA_ROWS_PER_STEP // NA_GROUP_ROWS
    scores_and_max(0)

    def body(g, carry):
        scores_and_max(g + 1)
        softmax_and_values(g)
        return carry

    lax.fori_loop(0, n_groups - 1, body, 0)
    softmax_and_values(n_groups - 1)


def _na_attention(qkv, bias_tab, batch, seq, d):
    n = qkv.shape[0]
    n_rows = seq // GRID_W
    chunks = d // 128
    step_tokens = NA_ROWS_PER_STEP * GRID_W
    rsteps = n_rows // NA_ROWS_PER_STEP
    nrow = bias_tab.shape[2]
    nkey = bias_tab.shape[3]
    return pl.pallas_call(
        functools.partial(_na_kernel, n_rows=n_rows),
        grid=(batch, chunks, rsteps),
        in_specs=[
            pl.BlockSpec((step_tokens, 128), lambda b, c, r: (b * rsteps + r, c)),
            pl.BlockSpec((seq, 128), lambda b, c, r: (b, chunks + c)),
            pl.BlockSpec((seq, 128), lambda b, c, r: (b, 2 * chunks + c)),
            pl.BlockSpec((None, NA_WIN_ROWS, nrow, nkey), lambda b, c, r: (c, 0, 0, 0)),
        ],
        out_specs=pl.BlockSpec((step_tokens, 128), lambda b, c, r: (b * rsteps + r, c)),
        out_shape=jax.ShapeDtypeStruct((n, d), BF16),
        scratch_shapes=[
            pltpu.VMEM((NA_ROWS_PER_STEP, nrow, nkey), F32),
            pltpu.VMEM((NA_ROWS_PER_STEP, nrow, 128), F32),
            pltpu.VMEM((NA_ROWS_PER_STEP, nrow, nkey), BF16),
        ],
        compiler_params=_cparams(("arbitrary", "arbitrary", "arbitrary")),
        name="na_attention",
    )(qkv, qkv, qkv, bias_tab)


def _log_sigmoid(v):
    return jnp.minimum(v, 0.0) - jnp.log(1.0 + jnp.exp(-jnp.abs(v)))


def _gla_kernel(q_ref, k_ref, v_ref, r_ref, a_ref, wg_ref, bg_ref, gn_ref, o_ref,
                st_ref, of_ref, cum_ref, qe_ref, ke_ref, kdt_ref, att_ref, av_ref, u_ref, dec_ref, os_ref,
                *, cb, nb):
    p = pl.program_id(2)
    i = pl.program_id(3)
    fwd = p == 0
    step_rows = cb * GLA_CHUNK

    @pl.when(i == 0)
    def _():
        st_ref[...] = jnp.zeros_like(st_ref)

    blk = jnp.where(fwd, i, nb - 1 - i)
    tr = lax.broadcasted_iota(jnp.int32, (GLA_CHUNK, GLA_CHUNK), 0)
    tc = lax.broadcasted_iota(jnp.int32, (GLA_CHUNK, GLA_CHUNK), 1)
    tmask = (tr - tc) * (1 - 2 * p) >= 0
    tmat = tmask.astype(BF16)
    nt = (((1,), (1,)), ((), ()))

    gl = jnp.dot(a_ref[...], wg_ref[...], preferred_element_type=F32) + bg_ref[...]
    g = _log_sigmoid(gl) * (1.0 / GLA_GATE_NORM)
    g1 = g.astype(BF16)
    e1 = g - g1.astype(F32)
    g2 = e1.astype(BF16)
    g3 = (e1 - g2.astype(F32)).astype(BF16)
    for c in range(cb):
        rows = slice(c * GLA_CHUNK, (c + 1) * GLA_CHUNK)
        cum_ref[rows, :] = (jnp.dot(tmat, g1[rows], preferred_element_type=F32)
                            + jnp.dot(tmat, g2[rows], preferred_element_type=F32)
                            + jnp.dot(tmat, g3[rows], preferred_element_type=F32))
    for c in range(cb):
        rows = slice(c * GLA_CHUNK, (c + 1) * GLA_CHUNK)
        cum = cum_ref[rows, :]
        last = jnp.where(fwd, cum[GLA_CHUNK - 1:GLA_CHUNK], cum[0:1])
        q = q_ref[rows, :].astype(F32)
        k = k_ref[rows, :].astype(F32)
        qe_ref[c] = (q * jnp.exp(cum)).astype(BF16)
        ke_ref[c] = (k * jnp.exp(-cum)).astype(BF16)
        kdt_ref[c] = (k * jnp.exp(last - cum)).T.astype(BF16)
        dcol = jnp.broadcast_to(jnp.exp(last), (8, last.shape[1])).T
        dec_ref[c] = jnp.broadcast_to(dcol[:, 0:1], dec_ref.shape[1:])
    for c in range(cb):
        att = lax.dot_general(qe_ref[c], ke_ref[c], nt, preferred_element_type=F32)
        att_ref[c] = jnp.where(tmask, att, 0.0).astype(BF16)
    for c in range(cb):
        rows = slice(c * GLA_CHUNK, (c + 1) * GLA_CHUNK)
        v = v_ref[rows, :]
        av_ref[c] = jnp.dot(att_ref[c], v, preferred_element_type=F32)
        u_ref[c] = jnp.dot(kdt_ref[c], v, preferred_element_type=F32)

    lane_tiles = st_ref.shape[1] // 128
    for cc in range(cb):
        c = jnp.where(fwd, cc, cb - 1 - cc)
        st = st_ref[...]
        o = av_ref[c] + jnp.dot(qe_ref[c], st.astype(BF16), preferred_element_type=F32)
        dec = dec_ref[c]
        st_ref[...] = st * jnp.concatenate([dec] * lane_tiles, axis=1) + u_ref[c]
        os_ref[pl.ds(pl.multiple_of(c * GLA_CHUNK, GLA_CHUNK), GLA_CHUNK), :] = o

    g0 = pl.multiple_of(blk * step_rows, step_rows)

    @pl.when(fwd)
    def _():
        of_ref[pl.ds(g0, step_rows), :] = os_ref[...]

    @pl.when(jnp.logical_not(fwd))
    def _():
        ot = of_ref[pl.ds(g0, step_rows), :] + os_ref[...]
        ms = jnp.mean(ot * ot, axis=-1, keepdims=True)
        y = ot * lax.rsqrt(ms + EPS) * gn_ref[...]
        y = y * _silu(r_ref[...].astype(F32))
        o_ref[...] = y.astype(o_ref.dtype)


def _gla(proj, a_low, wg_pad, b_gate, gn_g, batch, seq, d):
    n = proj.shape[0]
    hk = d // 2 // GLA_HEADS
    hv = d // GLA_HEADS
    cb = min(GLA_CHUNKS_PER_STEP, seq // GLA_CHUNK)
    step_rows = cb * GLA_CHUNK
    nb = seq // step_rows

    def rowblk(b, p, i):
        return b * nb + jnp.where(p == 0, i, nb - 1 - i)

    def outblk(b, p, i):
        return b * nb + jnp.where(p == 0, nb - 1, nb - 1 - i)

    return pl.pallas_call(
        functools.partial(_gla_kernel, cb=cb, nb=nb),
        grid=(batch, GLA_HEADS, 2, nb),
        in_specs=[
            pl.BlockSpec((step_rows, hk), lambda b, h, p, i: (rowblk(b, p, i), h)),
            pl.BlockSpec((step_rows, hk), lambda b, h, p, i: (rowblk(b, p, i), GLA_HEADS + h)),
            pl.BlockSpec((step_rows, hv), lambda b, h, p, i: (rowblk(b, p, i), GLA_HEADS + h)),
            pl.BlockSpec((step_rows, hv), lambda b, h, p, i: (rowblk(b, p, i), 2 * GLA_HEADS + h)),
            pl.BlockSpec((step_rows, 128), lambda b, h, p, i: (rowblk(b, p, i), 0)),
            pl.BlockSpec((None, 128, hk), lambda b, h, p, i: (p, 0, h)),
            pl.BlockSpec((None, 1, hk), lambda b, h, p, i: (p, 0, h)),
            pl.BlockSpec((1, hv), lambda b, h, p, i: (0, h)),
        ],
        out_specs=pl.BlockSpec((step_rows, hv), lambda b, h, p, i: (outblk(b, p, i), h)),
        out_shape=jax.ShapeDtypeStruct((n, d), BF16),
        scratch_shapes=[
            pltpu.VMEM((hk, hv), F32),
            pltpu.VMEM((seq, hv), F32),
            pltpu.VMEM((step_rows, hk), F32),
            pltpu.VMEM((cb, GLA_CHUNK, hk), BF16),
            pltpu.VMEM((cb, GLA_CHUNK, hk), BF16),
            pltpu.VMEM((cb, hk, GLA_CHUNK), BF16),
            pltpu.VMEM((cb, GLA_CHUNK, GLA_CHUNK), BF16),
            pltpu.VMEM((cb, GLA_CHUNK, hv), F32),
            pltpu.VMEM((cb, hk, hv), F32),
            pltpu.VMEM((cb, hk, 128), F32),
            pltpu.VMEM((step_rows, hv), F32),
        ],
        compiler_params=_cparams(("arbitrary", "arbitrary", "arbitrary", "arbitrary")),
        name="gla",
    )(proj, proj, proj, proj, a_low, wg_pad, b_gate.reshape(2, 1, -1), gn_g.reshape(1, d))


def _router_kernel(x_ref, g_ref, sc_ref, sh_ref, w_ref, h_ref, eid_ref, rt_ref, cnt_ref, run_ref):
    @pl.when(pl.program_id(0) == 0)
    def _():
        run_ref[...] = jnp.zeros_like(run_ref)

    h = _norm_mod(x_ref[...], g_ref[...], sc_ref[...], sh_ref[...])
    hb = h.astype(BF16)
    half = h.shape[1] // 2
    bits = lax.bitcast_convert_type(hb.astype(F32), jnp.uint32)
    h_ref[...] = (bits[:, :half] >> 16) | (bits[:, half:] & jnp.uint32(0xFFFF0000))
    logits = jnp.dot(hb, w_ref[...], preferred_element_type=F32)
    lane = lax.broadcasted_iota(jnp.int32, logits.shape, 1)
    gl = jnp.where(lane < MOE_GROUPS, logits, MASK_VALUE)
    gmax = jnp.max(gl, axis=-1, keepdims=True)
    gidx = jnp.min(jnp.where(gl == gmax, lane, ROUTER_LANES), axis=-1, keepdims=True)
    g_w = 1.0 / jnp.sum(jnp.exp(gl - gmax), axis=-1, keepdims=True)
    e_lane = lane - MOE_GROUPS
    in_group = (e_lane >= 0) & (e_lane < MOE_EXPERTS) & ((e_lane >> 3) == gidx)
    el = jnp.where(in_group, logits, MASK_VALUE)
    m1 = jnp.max(el, axis=-1, keepdims=True)
    i1 = jnp.min(jnp.where(el == m1, lane, ROUTER_LANES), axis=-1, keepdims=True)
    el2 = jnp.where(lane == i1, MASK_VALUE, el)
    m2 = jnp.max(el2, axis=-1, keepdims=True)
    i2 = jnp.min(jnp.where(el2 == m2, lane, ROUTER_LANES), axis=-1, keepdims=True)
    t = jnp.exp(m2 - m1)
    w1 = g_w / (1.0 + t)
    w2 = w1 * t
    rt_ref[...] = jnp.where(lane == 0, w1, jnp.where(lane == 1, w2, 0.0))
    tm = logits.shape[0]
    oh1 = lane == i1
    oh2 = lane == i2
    both = (oh1 | oh2).astype(BF16)
    tr = lax.broadcasted_iota(jnp.int32, (tm, tm), 0)
    tc = lax.broadcasted_iota(jnp.int32, (tm, tm), 1)
    before = (tc < tr).astype(BF16)
    prior = jnp.dot(before, both, preferred_element_type=F32) + run_ref[0:1, :]
    r1 = jnp.sum(jnp.where(oh1, prior, 0.0), axis=-1, keepdims=True).astype(jnp.int32)
    r2 = jnp.sum(jnp.where(oh2, prior, 0.0), axis=-1, keepdims=True).astype(jnp.int32)
    eid_ref[...] = jnp.where(lane == 0, i1 - MOE_GROUPS,
                             jnp.where(lane == 1, i2 - MOE_GROUPS,
                                       jnp.where(lane == 2, r1, jnp.where(lane == 3, r2, 0))))
    run = run_ref[...] + jnp.sum(both.astype(F32), axis=0, keepdims=True)
    run_ref[...] = run
    cnt_ref[...] = run


def _router(x, g, sc, sh, w_router, seq):
    n, d = x.shape
    tm = min(512, seq)
    return pl.pallas_call(
        _router_kernel,
        grid=(n // tm,),
        in_specs=[
            pl.BlockSpec((tm, d), lambda i: (i, 0)),
            pl.BlockSpec((1, d), lambda i: (0, 0)),
            pl.BlockSpec((None, 1, d), lambda i: (i * tm // seq, 0, 0)),
            pl.BlockSpec((None, 1, d), lambda i: (i * tm // seq, 0, 0)),
            pl.BlockSpec((d, ROUTER_LANES), lambda i: (0, 0)),
        ],
        out_specs=[
            pl.BlockSpec((tm, d // 2), lambda i: (i, 0)),
            pl.BlockSpec((tm, ROUTER_LANES), lambda i: (i, 0)),
            pl.BlockSpec((tm, ROUTER_LANES), lambda i: (i, 0)),
            pl.BlockSpec((8, ROUTER_LANES), lambda i: (0, 0)),
        ],
        out_shape=[
            jax.ShapeDtypeStruct((n, d // 2), jnp.uint32),
            jax.ShapeDtypeStruct((n, ROUTER_LANES), jnp.int32),
            jax.ShapeDtypeStruct((n, ROUTER_LANES), F32),
            jax.ShapeDtypeStruct((8, ROUTER_LANES), F32),
        ],
        scratch_shapes=[pltpu.VMEM((8, ROUTER_LANES), F32)],
        compiler_params=_cparams(("arbitrary",)),
        name="moe_router",
    )(x, g.reshape(1, d), sc, sh, w_router)


def _dispatch_tables(eid, cnt):
    n = eid.shape[0]
    a = n * 2
    flat_e = eid[:, 0:2].reshape(a)
    rank = eid[:, 2:4].reshape(a)
    counts = cnt[0, MOE_GROUPS:MOE_GROUPS + MOE_EXPERTS].astype(jnp.int32)
    padded = (counts + MOE_BLOCK - 1) // MOE_BLOCK * MOE_BLOCK
    pend = jnp.cumsum(padded)
    pstart = pend - padded
    onehot = flat_e[:, None] == jnp.arange(MOE_EXPERTS, dtype=jnp.int32)[None, :]
    dest = jnp.sum(jnp.where(onehot, pstart[None, :], 0), axis=1) + rank
    n_blocks = (a + MOE_EXPERTS * MOE_BLOCK) // MOE_BLOCK
    bstart = jnp.arange(n_blocks, dtype=jnp.int32) * MOE_BLOCK
    block_expert = jnp.minimum(jnp.sum((pend[None, :] <= bstart[:, None]).astype(jnp.int32), axis=1),
                               MOE_EXPERTS - 1)
    nvalid = (pend[-1] // MOE_BLOCK).astype(jnp.int32).reshape(1)
    dest2 = dest.reshape(n, 2)
    return block_expert, nvalid, dest2[:, 0], dest2[:, 1], pstart + counts, pend


def _moe_kernel(be_ref, d0_ref, d1_ref, nv_ref, plo_ref, phi_ref, h_hbm, wup_ref, wdn_ref, o_ref,
                tok_ref, xbuf, xb_ref, wup_bf, wdn_bf, sem, *, tb, ff):
    b = pl.program_id(0)
    nvalid = nv_ref[0]
    n_tok = d0_ref.shape[0]

    def row_copy(tok, i, slot):
        return pltpu.make_async_copy(h_hbm.at[pl.ds(tok, 1), :], xbuf.at[slot, pl.ds(i, 1), :],
                                     sem.at[slot])

    def wait_rows(slot):
        pltpu.make_async_copy(h_hbm.at[pl.ds(0, tb), :], xbuf.at[slot], sem.at[slot]).wait()

    @pl.when(b == 0)
    def _():
        def fill(i, carry):
            for u in range(MOE_FILL_UNROLL):
                t = i * MOE_FILL_UNROLL + u
                tok_ref[d0_ref[t]] = t
                tok_ref[d1_ref[t]] = t
            return carry
        lax.fori_loop(0, n_tok // MOE_FILL_UNROLL, fill, 0)

        def pad_expert(e, carry):
            def pad(s, c):
                tok_ref[s] = 0
                return c
            lax.fori_loop(plo_ref[e], phi_ref[e], pad, 0)
            return carry
        lax.fori_loop(0, MOE_EXPERTS, pad_expert, 0)

        for ahead in range(MOE_GATHER_DEPTH - 1):
            base = jnp.minimum(ahead, nvalid - 1) * tb

            def body(i, carry, base=base, ahead=ahead):
                row_copy(tok_ref[base + i], i, ahead).start()
                return carry
            lax.fori_loop(0, tb, body, 0)

    prev = be_ref[jnp.maximum(b - 1, 0)]

    @pl.when((b < nvalid) & ((b == 0) | (be_ref[b] != prev)))
    def _():
        wup_bf[...] = wup_ref[...].astype(BF16)
        wdn_bf[...] = wdn_ref[...].astype(BF16)

    @pl.when(b < nvalid)
    def _():
        slot = lax.rem(b, MOE_GATHER_DEPTH)
        wait_rows(slot)
        xp = xbuf[slot]
        half = xp.shape[1]
        xb_ref[:, :half] = lax.bitcast_convert_type(xp << 16, F32).astype(BF16)
        xb_ref[:, half:] = lax.bitcast_convert_type(xp & jnp.uint32(0xFFFF0000), F32).astype(BF16)
        ahead = MOE_GATHER_DEPTH - 1
        nslot = lax.rem(b + ahead, MOE_GATHER_DEPTH)
        base = jnp.minimum(b + ahead, nvalid - 1) * tb
        for i in range(tb):
            row_copy(tok_ref[base + i], i, nslot).start()
        hcat = jnp.dot(xb_ref[...], wup_bf[...], preferred_element_type=F32)
        act = (_silu(hcat[:, :ff]) * hcat[:, ff:]).astype(BF16)
        o_ref[...] = jnp.dot(act, wdn_bf[...], preferred_element_type=F32)

        @pl.when(b == nvalid - 1)
        def _():
            for k in range(1, MOE_GATHER_DEPTH):
                wait_rows(lax.rem(b + k, MOE_GATHER_DEPTH))

    @pl.when(b >= nvalid)
    def _():
        o_ref[...] = jnp.zeros_like(o_ref)


def _moe_experts(h2, block_expert, d0, d1, nvalid, pad_lo, pad_hi, w_up, w_down, layer):
    n = h2.shape[0]
    d = 2 * h2.shape[1]
    tb = MOE_BLOCK
    n_blocks = block_expert.shape[0]
    n_slots = n_blocks * tb
    ff = w_down.shape[2]
    grid_spec = pltpu.PrefetchScalarGridSpec(
        num_scalar_prefetch=6,
        grid=(n_blocks,),
        in_specs=[
            pl.BlockSpec(memory_space=pl.ANY),
            pl.BlockSpec((None, None, d, 2 * ff), lambda b, be, *_: (layer, be[b], 0, 0)),
            pl.BlockSpec((None, None, ff, d), lambda b, be, *_: (layer, be[b], 0, 0)),
        ],
        out_specs=pl.BlockSpec((tb, d), lambda b, be, *_: (b, 0)),
        scratch_shapes=[
            pltpu.SMEM((n_slots,), jnp.int32),
            pltpu.VMEM((MOE_GATHER_DEPTH, tb, d // 2), jnp.uint32),
            pltpu.VMEM((tb, d), BF16),
            pltpu.VMEM((d, 2 * ff), BF16),
            pltpu.VMEM((ff, d), BF16),
            pltpu.SemaphoreType.DMA((MOE_GATHER_DEPTH,)),
        ],
    )
    return pl.pallas_call(
        functools.partial(_moe_kernel, tb=tb, ff=ff),
        grid_spec=grid_spec,
        out_shape=jax.ShapeDtypeStruct((n_slots, d), F32),
        compiler_params=_cparams(("arbitrary",)),
        name="moe_experts",
    )(block_expert, d0, d1, nvalid, pad_lo, pad_hi, h2, w_up, w_down)


def _combine_kernel(d0_ref, d1_ref, yb_hbm, x_ref, rt_ref, gate_ref, *rest, tm, tail):
    if tail == "codes":
        g_ref, sc_ref, sh_ref, wa_ref, o_ref, a_ref, ybuf, ys_ref, sem = rest
    elif tail == "final":
        g_ref, o_ref, ybuf, ys_ref, sem = rest
    else:
        o_ref, ybuf, ys_ref, sem = rest
    t = pl.program_id(0)
    nt = pl.num_programs(0)

    def row_copy(src_row, i, slot):
        return pltpu.make_async_copy(yb_hbm.at[pl.ds(src_row, 1), :], ybuf.at[slot, pl.ds(i, 1), :],
                                     sem.at[slot])

    def wait_rows(slot):
        pltpu.make_async_copy(yb_hbm.at[pl.ds(0, 2 * tm), :], ybuf.at[slot], sem.at[slot]).wait()

    def gather(blk, slot):
        base = blk * tm
        for i in range(tm):
            row_copy(d0_ref[base + i], i, slot).start()
            row_copy(d1_ref[base + i], tm + i, slot).start()

    ahead = MOE_GATHER_DEPTH - 1

    @pl.when(t == 0)
    def _():
        for k in range(ahead):
            gather(jnp.minimum(k, nt - 1), k)

    slot = lax.rem(t, MOE_GATHER_DEPTH)
    wait_rows(slot)
    rt = rt_ref[...]
    w0 = jnp.broadcast_to(rt[:, 0:1], (tm, 128))
    w1 = jnp.broadcast_to(rt[:, 1:2], (tm, 128))
    for c in range(ys_ref.shape[1] // 128):
        cols = pl.ds(c * 128, 128)
        ys_ref[:, cols] = w0 * ybuf[slot, pl.ds(0, tm), cols] + w1 * ybuf[slot, pl.ds(tm, tm), cols]
    gather(jnp.minimum(t + ahead, nt - 1), lax.rem(t + ahead, MOE_GATHER_DEPTH))
    xn = x_ref[...] + gate_ref[...] * ys_ref[...]
    if tail == "final":
        ms = jnp.mean(xn * xn, axis=-1, keepdims=True)
        o_ref[...] = xn * lax.rsqrt(ms + EPS) * g_ref[...]
    else:
        o_ref[...] = xn
    if tail == "codes":
        h = _norm_mod(xn, g_ref[...], sc_ref[...], sh_ref[...]).astype(BF16)
        a_ref[...] = jnp.dot(h, wa_ref[...], preferred_element_type=F32).astype(a_ref.dtype)

    @pl.when(t == nt - 1)
    def _():
        for k in range(1, MOE_GATHER_DEPTH):
            wait_rows(lax.rem(t + k, MOE_GATHER_DEPTH))


def _combine(yb, d0, d1, x, route, gate, seq, tail="plain", norm=None):
    n, d = x.shape
    tm = min(256, seq)
    row = lambda t, d0, d1: (t, 0)
    const = lambda t, d0, d1: (0, 0)
    per_batch = lambda t, d0, d1: (t * tm // seq, 0, 0)
    in_specs = [
        pl.BlockSpec(memory_space=pl.ANY),
        pl.BlockSpec((tm, d), row),
        pl.BlockSpec((tm, ROUTER_LANES), row),
        pl.BlockSpec((None, 1, d), per_batch),
    ]
    args = [yb, x, route, gate]
    out_specs = pl.BlockSpec((tm, d), row)
    out_shape = jax.ShapeDtypeStruct((n, d), F32)
    if tail == "codes":
        g, sc, sh, w_a = norm
        in_specs += [pl.BlockSpec((1, d), const), pl.BlockSpec((None, 1, d), per_batch),
                     pl.BlockSpec((None, 1, d), per_batch), pl.BlockSpec((d, 128), const)]
        args += [g.reshape(1, d), sc, sh, w_a]
        out_specs = [out_specs, pl.BlockSpec((tm, 128), row)]
        out_shape = [out_shape, jax.ShapeDtypeStruct((n, 128), BF16)]
    elif tail == "final":
        in_specs += [pl.BlockSpec((1, d), const)]
        args += [norm[0].reshape(1, d)]
    grid_spec = pltpu.PrefetchScalarGridSpec(
        num_scalar_prefetch=2,
        grid=(n // tm,),
        in_specs=in_specs,
        out_specs=out_specs,
        scratch_shapes=[pltpu.VMEM((MOE_GATHER_DEPTH, 2 * tm, d), F32),
                        pltpu.VMEM((tm, d), F32),
                        pltpu.SemaphoreType.DMA((MOE_GATHER_DEPTH,))],
    )
    return pl.pallas_call(
        functools.partial(_combine_kernel, tm=tm, tail=tail),
        grid_spec=grid_spec,
        out_shape=out_shape,
        compiler_params=_cparams(("arbitrary",)),
        name="moe_combine",
    )(d0, d1, *args)


def _proj_tile(nout):
    for tn in (1536, 896, 768, 512, 256, 128):
        if nout % tn == 0:
            return tn
    raise ValueError(f"no projection tile for width {nout}")


def kernel(x, c, ada_w, ada_b, norm1_g, norm2_g, na_w_in, na_w_out, na_rpb, gla_w_in, gla_w_gate_up, gla_b_gate, gla_gn_g, gla_w_out, moe_w_router_group, moe_w_router_expert, moe_w_up, moe_w_down, final_g):
    batch, seq, d = x.shape
    depth = ada_w.shape[0]
    n = batch * seq
    assert seq % (GRID_W * NA_ROWS_PER_STEP) == 0 and d % 1024 == 0
    dk = d // 2

    mod = _adaln(c, ada_w, ada_b)
    xf = x.reshape(n, d)
    i_na = 0
    i_gla = 0
    for i in range(depth):
        sh1, sc1, gt1, sh2, sc2, gt2 = [m.reshape(batch, 1, d) for m in jnp.split(mod[i], 6, axis=-1)]
        if i % 2 == 0:
            w_in = na_w_in[i_na].astype(BF16)
            colscale = jnp.concatenate([jnp.full((d,), NA_HEAD_DIM ** -0.5, F32), jnp.ones((2 * d,), F32)])
            qkv = _norm_mod_matmul(xf, norm1_g[i], sc1, sh1, w_in, colscale, seq, _proj_tile(3 * d))
            bias_tab = _na_bias_table(na_rpb[i_na])
            y = _na_attention(qkv, bias_tab, batch, seq, d)
            w_out = na_w_out[i_na].astype(BF16)
            i_na += 1
        else:
            hk = dk // GLA_HEADS
            w_in = gla_w_in[i_gla][:, :3 * d].astype(BF16)
            colscale = jnp.concatenate([jnp.full((dk,), hk ** -0.5, F32), jnp.ones((3 * d - dk,), F32)])
            proj = _norm_mod_matmul(xf, norm1_g[i], sc1, sh1, w_in, colscale, seq, _proj_tile(3 * d))
            wg_pad = jnp.zeros((2, 128, dk), F32)
            wg_pad = wg_pad.at[0, :GLA_GATE_RANK].set(gla_w_gate_up[i_gla, 0])
            wg_pad = wg_pad.at[1, GLA_GATE_RANK:2 * GLA_GATE_RANK].set(gla_w_gate_up[i_gla, 1])
            y = _gla(proj, a_low, wg_pad.astype(BF16), gla_b_gate[i_gla], gla_gn_g[i_gla], batch, seq, d)
            w_out = gla_w_out[i_gla].astype(BF16)
            i_gla += 1
        xf = _matmul_residual(y, w_out, xf, gt1, seq)

        w_router = jnp.concatenate(
            [moe_w_router_group[i], moe_w_router_expert[i],
             jnp.zeros((d, ROUTER_LANES - MOE_GROUPS - MOE_EXPERTS), F32)], axis=1).astype(BF16)
        h2, eid, route, cnt = _router(xf, norm2_g[i], sc2, sh2, w_router, seq)
        block_expert, nvalid, d0, d1, pad_lo, pad_hi = _dispatch_tables(eid, cnt)
        yb = _moe_experts(h2, block_expert, d0, d1, nvalid, pad_lo, pad_hi, moe_w_up, moe_w_down, i)
        if i == depth - 1:
            xf = _combine(yb, d0, d1, xf, route, gt2, seq, "final", (final_g,))
        elif (i + 1) % 2 == 1:
            nsh1, nsc1 = [m.reshape(batch, 1, d) for m in jnp.split(mod[i + 1], 6, axis=-1)[:2]]
            w_a = jnp.concatenate([gla_w_in[i_gla][:, 3 * d:],
                                   jnp.zeros((d, 128 - 2 * GLA_GATE_RANK), F32)], axis=1).astype(BF16)
            xf, a_low = _combine(yb, d0, d1, xf, route, gt2, seq, "codes",
                                 (norm1_g[i + 1], nsc1, nsh1, w_a))
        else:
            xf = _combine(yb, d0, d1, xf, route, gt2, seq)

    return xf.reshape(batch, seq, d)
```

```python
import functools

import jax
import jax.numpy as jnp
import numpy as np
from jax import lax
from jax.experimental import pallas as pl
from jax.experimental.pallas import tpu as pltpu

F32 = jnp.float32
BF16 = jnp.bfloat16

EPS = 1e-6
GRID_W = 64
NA_HEAD_DIM = 32
NA_WIN_ROWS = 8
NA_WIN_COLS = 16
NA_HEADS_PER_CHUNK = 4
NA_QBLOCKS = ((0, 24, 0), (24, 16, 16), (40, 24, 32))
NA_KEY_COLS = 32
NA_ROWS_PER_STEP = 32
NA_GROUP_ROWS = 32
MASK_VALUE = -1e30

GLA_HEADS = 4
GLA_GATE_RANK = 16
GLA_GATE_NORM = 16.0
GLA_CHUNK = 64
GLA_CHUNKS_PER_STEP = 16

MOE_GROUPS = 4
MOE_EXPERTS_PER_GROUP = 8
MOE_EXPERTS = MOE_GROUPS * MOE_EXPERTS_PER_GROUP
MOE_BLOCK = 256
MOE_GATHER_DEPTH = 3
MOE_FILL_UNROLL = 8
ROUTER_LANES = 128

VMEM_LIMIT = 56 * 1024 * 1024


def _cparams(sem):
    return pltpu.CompilerParams(dimension_semantics=sem, vmem_limit_bytes=VMEM_LIMIT)


def _silu(v):
    return v * jax.nn.sigmoid(v)


def _adaln_kernel(c_ref, w_ref, b_ref, o_ref):
    cond = _silu(c_ref[...]).astype(BF16)
    w = w_ref[...].astype(BF16)
    o_ref[...] = jnp.dot(cond, w, preferred_element_type=F32) + b_ref[...]


def _adaln(c, ada_w, ada_b):
    depth, d, n6 = ada_w.shape
    b = c.shape[0]
    rows = 16
    cpad = jnp.zeros((rows, d), F32).at[:b].set(c)
    tn = 1024
    out = pl.pallas_call(
        _adaln_kernel,
        grid=(depth, n6 // tn),
        in_specs=[
            pl.BlockSpec((rows, d), lambda i, j: (0, 0)),
            pl.BlockSpec((None, d, tn), lambda i, j: (i, 0, j)),
            pl.BlockSpec((None, 1, tn), lambda i, j: (i, 0, j)),
        ],
        out_specs=pl.BlockSpec((None, rows, tn), lambda i, j: (i, 0, j)),
        out_shape=jax.ShapeDtypeStruct((depth, rows, n6), F32),
        compiler_params=_cparams(("arbitrary", "arbitrary")),
        name="adaln",
    )(cpad, ada_w, ada_b.reshape(depth, 1, n6))
    return out[:, :b]


def _norm_mod(x, g, sc, sh):
    ms = jnp.mean(x * x, axis=-1, keepdims=True)
    y = x * lax.rsqrt(ms + EPS) * g
    return y * (1.0 + sc) + sh


def _nmm_kernel(x_ref, g_ref, sc_ref, sh_ref, w_ref, cs_ref, o_ref, h_ref):
    @pl.when(pl.program_id(1) == 0)
    def _():
        h_ref[...] = _norm_mod(x_ref[...], g_ref[...], sc_ref[...], sh_ref[...]).astype(BF16)

    acc = jnp.dot(h_ref[...], w_ref[...], preferred_element_type=F32)
    o_ref[...] = (acc * cs_ref[...]).astype(o_ref.dtype)


def _norm_mod_matmul(x, g, sc, sh, w, colscale, seq, tn):
    n, d = x.shape
    nout = w.shape[1]
    tm = min(1024, seq)
    return pl.pallas_call(
        _nmm_kernel,
        grid=(n // tm, nout // tn),
        in_specs=[
            pl.BlockSpec((tm, d), lambda i, j: (i, 0)),
            pl.BlockSpec((1, d), lambda i, j: (0, 0)),
            pl.BlockSpec((None, 1, d), lambda i, j: (i * tm // seq, 0, 0)),
            pl.BlockSpec((None, 1, d), lambda i, j: (i * tm // seq, 0, 0)),
            pl.BlockSpec((d, tn), lambda i, j: (0, j)),
            pl.BlockSpec((1, tn), lambda i, j: (0, j)),
        ],
        out_specs=pl.BlockSpec((tm, tn), lambda i, j: (i, j)),
        out_shape=jax.ShapeDtypeStruct((n, nout), BF16),
        scratch_shapes=[pltpu.VMEM((tm, d), BF16)],
        compiler_params=_cparams(("arbitrary", "arbitrary")),
        name="norm_mod_matmul",
    )(x, g.reshape(1, d), sc, sh, w, colscale.reshape(1, nout))


def _mmres_kernel(a_ref, w_ref, res_ref, gate_ref, o_ref):
    acc = jnp.dot(a_ref[...], w_ref[...], preferred_element_type=F32)
    o_ref[...] = res_ref[...] + gate_ref[...] * acc


def _matmul_residual(a, w, res, gate, seq):
    n, k = a.shape
    d = w.shape[1]
    tm = min(512, seq)
    return pl.pallas_call(
        _mmres_kernel,
        grid=(n // tm,),
        in_specs=[
            pl.BlockSpec((tm, k), lambda i: (i, 0)),
            pl.BlockSpec((k, d), lambda i: (0, 0)),
            pl.BlockSpec((tm, d), lambda i: (i, 0)),
            pl.BlockSpec((None, 1, d), lambda i: (i * tm // seq, 0, 0)),
        ],
        out_specs=pl.BlockSpec((tm, d), lambda i: (i, 0)),
        out_shape=jax.ShapeDtypeStruct((n, d), F32),
        compiler_params=_cparams(("arbitrary",)),
        name="matmul_residual",
    )(a, w, res, gate)


def _na_bias_table(rpb):
    heads = rpb.shape[0]
    chunks = heads // NA_HEADS_PER_CHUNK
    wr, wc = NA_WIN_ROWS, NA_WIN_COLS
    rpb = rpb.astype(F32)
    a = jnp.stack([rpb[:, wr - 1 - di:2 * wr - 1 - di, :] for di in range(wr)], axis=1)
    blocks = []
    for qs, nq, ks in NA_QBLOCKS:
        cols = []
        for q in range(qs, qs + nq):
            cstart = min(max(q - wc // 2, 0), GRID_W - wc)
            first = cstart - q + wc - 1
            off = cstart - ks
            cols.append(jnp.pad(a[..., first:first + wc],
                                ((0, 0), (0, 0), (0, 0), (off, NA_KEY_COLS - wc - off)),
                                constant_values=MASK_VALUE))
        t = jnp.stack(cols, axis=3)
        t = t.reshape(chunks, NA_HEADS_PER_CHUNK, wr, wr, nq, NA_KEY_COLS)
        t = t.transpose(0, 2, 1, 4, 3, 5)
        blocks.append(t.reshape(chunks, wr, NA_HEADS_PER_CHUNK * nq, wr * NA_KEY_COLS))
    return jnp.concatenate(blocks, axis=2)


def _na_kernel(q_ref, k_ref, v_ref, bias_ref, o_ref, l_ref, m_ref, p_ref, *, n_rows):
    rblk = pl.program_id(2)
    lane = lax.broadcasted_iota(jnp.int32, (1, 128), 1)
    hmask = [(lane // NA_HEAD_DIM == hp).astype(F32) for hp in range(NA_HEADS_PER_CHUNK)]
    win_tokens = NA_WIN_ROWS * GRID_W
    nkeys = NA_WIN_ROWS * NA_KEY_COLS
    ones = jnp.ones((nkeys, 128), BF16)

    def window(rr):
        r = rblk * NA_ROWS_PER_STEP + rr
        rs = jnp.clip(r - NA_WIN_ROWS // 2, 0, n_rows - NA_WIN_ROWS)
        return pl.multiple_of(rs * GRID_W, GRID_W), r - rs

    def key_block(win, ks):
        return jnp.concatenate([win[i * GRID_W + ks:i * GRID_W + ks + NA_KEY_COLS]
                                for i in range(NA_WIN_ROWS)], axis=0)

    def scores_and_max(grp):
        for j in range(NA_GROUP_ROWS):
            rr = grp * NA_GROUP_ROWS + j
            w0, di = window(rr)
            q0 = pl.multiple_of(rr * GRID_W, GRID_W)
            qrow = q_ref[pl.ds(q0, GRID_W), :].astype(F32)
            kwin = k_ref[pl.ds(w0, win_tokens), :]
            row0 = 0
            for qs, nq, ks in NA_QBLOCKS:
                nst = NA_HEADS_PER_CHUNK * nq
                qb = qrow[qs:qs + nq]
                qstack = jnp.concatenate([qb * hmask[hp] for hp in range(NA_HEADS_PER_CHUNK)],
                                         axis=0).astype(BF16)
                logits = lax.dot_general(qstack, key_block(kwin, ks), (((1,), (1,)), ((), ())),
                                         preferred_element_type=F32)
                l_ref[rr, pl.ds(row0, nst), :] = logits + bias_ref[di, pl.ds(row0, nst), :]
                row0 += nst
        for j in range(NA_GROUP_ROWS):
            rr = grp * NA_GROUP_ROWS + j
            m = jnp.max(l_ref[rr], axis=-1, keepdims=True)
            m_ref[rr] = jnp.broadcast_to(m, m_ref.shape[1:])

    def softmax_and_values(grp):
        for j in range(NA_GROUP_ROWS):
            rr = grp * NA_GROUP_ROWS + j
            m = m_ref[rr]
            p_ref[rr] = jnp.exp(l_ref[rr] - jnp.concatenate([m, m], axis=1)).astype(BF16)
        for j in range(NA_GROUP_ROWS):
            rr = grp * NA_GROUP_ROWS + j
            w0, _ = window(rr)
            vwin = v_ref[pl.ds(w0, win_tokens), :]
            outs = []
            row0 = 0
            for qs, nq, ks in NA_QBLOCKS:
                nst = NA_HEADS_PER_CHUNK * nq
                v1 = jnp.concatenate([key_block(vwin, ks), ones], axis=1)
                acc = jnp.dot(p_ref[rr, pl.ds(row0, nst), :], v1, preferred_element_type=F32)
                num = acc[0:nq, :128] * hmask[0]
                den = acc[0:nq, 128:] * hmask[0]
                for hp in range(1, NA_HEADS_PER_CHUNK):
                    num = num + acc[hp * nq:(hp + 1) * nq, :128] * hmask[hp]
                    den = den + acc[hp * nq:(hp + 1) * nq, 128:] * hmask[hp]
                outs.append(num * (1.0 / den))
                row0 += nst
            orow = jnp.concatenate(outs, axis=0)
            o_ref[pl.ds(pl.multiple_of(rr * GRID_W, GRID_W), GRID_W), :] = orow.astype(o_ref.dtype)

    n_groups = NA_ROWS_PER_STEP // NA_GROUP_ROWS
    scores_and_max(0)

    def body(g, carry):
        scores_and_max(g + 1)
        softmax_and_values(g)
        return carry

    lax.fori_loop(0, n_groups - 1, body, 0)
    softmax_and_values(n_groups - 1)


def _na_attention(qkv, bias_tab, batch, seq, d):
    n = qkv.shape[0]
    n_rows = seq // GRID_W
    chunks = d // 128
    step_tokens = NA_ROWS_PER_STEP * GRID_W
    rsteps = n_rows // NA_ROWS_PER_STEP
    nrow = bias_tab.shape[2]
    nkey = bias_tab.shape[3]
    return pl.pallas_call(
        functools.partial(_na_kernel, n_rows=n_rows),
        grid=(batch, chunks, rsteps),
        in_specs=[
            pl.BlockSpec((step_tokens, 128), lambda b, c, r: (b * rsteps + r, c)),
            pl.BlockSpec((seq, 128), lambda b, c, r: (b, chunks + c)),
            pl.BlockSpec((seq, 128), lambda b, c, r: (b, 2 * chunks + c)),
            pl.BlockSpec((None, NA_WIN_ROWS, nrow, nkey), lambda b, c, r: (c, 0, 0, 0)),
        ],
        out_specs=pl.BlockSpec((step_tokens, 128), lambda b, c, r: (b * rsteps + r, c)),
        out_shape=jax.ShapeDtypeStruct((n, d), BF16),
        scratch_shapes=[
            pltpu.VMEM((NA_ROWS_PER_STEP, nrow, nkey), F32),
            pltpu.VMEM((NA_ROWS_PER_STEP, nrow, 128), F32),
            pltpu.VMEM((NA_ROWS_PER_STEP, nrow, nkey), BF16),
        ],
        compiler_params=_cparams(("arbitrary", "arbitrary", "arbitrary")),
        name="na_attention",
    )(qkv, qkv, qkv, bias_tab)


def _log_sigmoid(v):
    return jnp.minimum(v, 0.0) - jnp.log(1.0 + jnp.exp(-jnp.abs(v)))


def _gla_kernel(q_ref, k_ref, v_ref, r_ref, a_ref, wg_ref, bg_ref, gn_ref, o_ref,
                st_ref, of_ref, cum_ref, qe_ref, ke_ref, kdt_ref, att_ref, av_ref, u_ref, dec_ref, os_ref,
                *, cb, nb):
    p = pl.program_id(2)
    i = pl.program_id(3)
    fwd = p == 0
    step_rows = cb * GLA_CHUNK

    @pl.when(i == 0)
    def _():
        st_ref[...] = jnp.zeros_like(st_ref)

    blk = jnp.where(fwd, i, nb - 1 - i)
    tr = lax.broadcasted_iota(jnp.int32, (GLA_CHUNK, GLA_CHUNK), 0)
    tc = lax.broadcasted_iota(jnp.int32, (GLA_CHUNK, GLA_CHUNK), 1)
    tmask = (tr - tc) * (1 - 2 * p) >= 0
    tmat = tmask.astype(BF16)
    nt = (((1,), (1,)), ((), ()))

    gl = jnp.dot(a_ref[...], wg_ref[...], preferred_element_type=F32) + bg_ref[...]
    g = _log_sigmoid(gl) * (1.0 / GLA_GATE_NORM)
    g1 = g.astype(BF16)
    e1 = g - g1.astype(F32)
    g2 = e1.astype(BF16)
    g3 = (e1 - g2.astype(F32)).astype(BF16)
    for c in range(cb):
        rows = slice(c * GLA_CHUNK, (c + 1) * GLA_CHUNK)
        cum_ref[rows, :] = (jnp.dot(tmat, g1[rows], preferred_element_type=F32)
                            + jnp.dot(tmat, g2[rows], preferred_element_type=F32)
                            + jnp.dot(tmat, g3[rows], preferred_element_type=F32))
    for c in range(cb):
        rows = slice(c * GLA_CHUNK, (c + 1) * GLA_CHUNK)
        cum = cum_ref[rows, :]
        last = jnp.where(fwd, cum[GLA_CHUNK - 1:GLA_CHUNK], cum[0:1])
        q = q_ref[rows, :].astype(F32)
        k = k_ref[rows, :].astype(F32)
        qe_ref[c] = (q * jnp.exp(cum)).astype(BF16)
        ke_ref[c] = (k * jnp.exp(-cum)).astype(BF16)
        kdt_ref[c] = (k * jnp.exp(last - cum)).T.astype(BF16)
        dcol = jnp.broadcast_to(jnp.exp(last), (8, last.shape[1])).T
        dec_ref[c] = jnp.broadcast_to(dcol[:, 0:1], dec_ref.shape[1:])
    for c in range(cb):
        att = lax.dot_general(qe_ref[c], ke_ref[c], nt, preferred_element_type=F32)
        att_ref[c] = jnp.where(tmask, att, 0.0).astype(BF16)
    for c in range(cb):
        rows = slice(c * GLA_CHUNK, (c + 1) * GLA_CHUNK)
        v = v_ref[rows, :]
        av_ref[c] = jnp.dot(att_ref[c], v, preferred_element_type=F32)
        u_ref[c] = jnp.dot(kdt_ref[c], v, preferred_element_type=F32)

    lane_tiles = st_ref.shape[1] // 128
    for cc in range(cb):
        c = jnp.where(fwd, cc, cb - 1 - cc)
        st = st_ref[...]
        o = av_ref[c] + jnp.dot(qe_ref[c], st.astype(BF16), preferred_element_type=F32)
        dec = dec_ref[c]
        st_ref[...] = st * jnp.concatenate([dec] * lane_tiles, axis=1) + u_ref[c]
        os_ref[pl.ds(pl.multiple_of(c * GLA_CHUNK, GLA_CHUNK), GLA_CHUNK), :] = o

    g0 = pl.multiple_of(blk * step_rows, step_rows)

    @pl.when(fwd)
    def _():
        of_ref[pl.ds(g0, step_rows), :] = os_ref[...]

    @pl.when(jnp.logical_not(fwd))
    def _():
        ot = of_ref[pl.ds(g0, step_rows), :] + os_ref[...]
        ms = jnp.mean(ot * ot, axis=-1, keepdims=True)
        y = ot * lax.rsqrt(ms + EPS) * gn_ref[...]
        y = y * _silu(r_ref[...].astype(F32))
        o_ref[...] = y.astype(o_ref.dtype)


def _gla(proj, a_low, wg_pad, b_gate, gn_g, batch, seq, d):
    n = proj.shape[0]
    hk = d // 2 // GLA_HEADS
    hv = d // GLA_HEADS
    cb = min(GLA_CHUNKS_PER_STEP, seq // GLA_CHUNK)
    step_rows = cb * GLA_CHUNK
    nb = seq // step_rows

    def rowblk(b, p, i):
        return b * nb + jnp.where(p == 0, i, nb - 1 - i)

    def outblk(b, p, i):
        return b * nb + jnp.where(p == 0, nb - 1, nb - 1 - i)

    return pl.pallas_call(
        functools.partial(_gla_kernel, cb=cb, nb=nb),
        grid=(batch, GLA_HEADS, 2, nb),
        in_specs=[
            pl.BlockSpec((step_rows, hk), lambda b, h, p, i: (rowblk(b, p, i), h)),
            pl.BlockSpec((step_rows, hk), lambda b, h, p, i: (rowblk(b, p, i), GLA_HEADS + h)),
            pl.BlockSpec((step_rows, hv), lambda b, h, p, i: (rowblk(b, p, i), GLA_HEADS + h)),
            pl.BlockSpec((step_rows, hv), lambda b, h, p, i: (rowblk(b, p, i), 2 * GLA_HEADS + h)),
            pl.BlockSpec((step_rows, 128), lambda b, h, p, i: (rowblk(b, p, i), 0)),
            pl.BlockSpec((None, 128, hk), lambda b, h, p, i: (p, 0, h)),
            pl.BlockSpec((None, 1, hk), lambda b, h, p, i: (p, 0, h)),
            pl.BlockSpec((1, hv), lambda b, h, p, i: (0, h)),
        ],
        out_specs=pl.BlockSpec((step_rows, hv), lambda b, h, p, i: (outblk(b, p, i), h)),
        out_shape=jax.ShapeDtypeStruct((n, d), BF16),
        scratch_shapes=[
            pltpu.VMEM((hk, hv), F32),
            pltpu.VMEM((seq, hv), F32),
            pltpu.VMEM((step_rows, hk), F32),
            pltpu.VMEM((cb, GLA_CHUNK, hk), BF16),
            pltpu.VMEM((cb, GLA_CHUNK, hk), BF16),
            pltpu.VMEM((cb, hk, GLA_CHUNK), BF16),
            pltpu.VMEM((cb, GLA_CHUNK, GLA_CHUNK), BF16),
            pltpu.VMEM((cb, GLA_CHUNK, hv), F32),
            pltpu.VMEM((cb, hk, hv), F32),
            pltpu.VMEM((cb, hk, 128), F32),
            pltpu.VMEM((step_rows, hv), F32),
        ],
        compiler_params=_cparams(("arbitrary", "arbitrary", "arbitrary", "arbitrary")),
        name="gla",
    )(proj, proj, proj, proj, a_low, wg_pad, b_gate.reshape(2, 1, -1), gn_g.reshape(1, d))


def _router_kernel(x_ref, g_ref, sc_ref, sh_ref, w_ref, h_ref, eid_ref, rt_ref, cnt_ref, run_ref):
    @pl.when(pl.program_id(0) == 0)
    def _():
        run_ref[...] = jnp.zeros_like(run_ref)

    h = _norm_mod(x_ref[...], g_ref[...], sc_ref[...], sh_ref[...])
    hb = h.astype(BF16)
    half = h.shape[1] // 2
    bits = lax.bitcast_convert_type(hb.astype(F32), jnp.uint32)
    h_ref[...] = (bits[:, :half] >> 16) | (bits[:, half:] & jnp.uint32(0xFFFF0000))
    logits = jnp.dot(hb, w_ref[...], preferred_element_type=F32)
    lane = lax.broadcasted_iota(jnp.int32, logits.shape, 1)
    gl = jnp.where(lane < MOE_GROUPS, logits, MASK_VALUE)
    gmax = jnp.max(gl, axis=-1, keepdims=True)
    gidx = jnp.min(jnp.where(gl == gmax, lane, ROUTER_LANES), axis=-1, keepdims=True)
    g_w = 1.0 / jnp.sum(jnp.exp(gl - gmax), axis=-1, keepdims=True)
    e_lane = lane - MOE_GROUPS
    in_group = (e_lane >= 0) & (e_lane < MOE_EXPERTS) & ((e_lane >> 3) == gidx)
    el = jnp.where(in_group, logits, MASK_VALUE)
    m1 = jnp.max(el, axis=-1, keepdims=True)
    i1 = jnp.min(jnp.where(el == m1, lane, ROUTER_LANES), axis=-1, keepdims=True)
    el2 = jnp.where(lane == i1, MASK_VALUE, el)
    m2 = jnp.max(el2, axis=-1, keepdims=True)
    i2 = jnp.min(jnp.where(el2 == m2, lane, ROUTER_LANES), axis=-1, keepdims=True)
    t = jnp.exp(m2 - m1)
    w1 = g_w / (1.0 + t)
    w2 = w1 * t
    rt_ref[...] = jnp.where(lane == 0, w1, jnp.where(lane == 1, w2, 0.0))
    tm = logits.shape[0]
    oh1 = lane == i1
    oh2 = lane == i2
    both = (oh1 | oh2).astype(BF16)
    tr = lax.broadcasted_iota(jnp.int32, (tm, tm), 0)
    tc = lax.broadcasted_iota(jnp.int32, (tm, tm), 1)
    before = (tc < tr).astype(BF16)
    prior = jnp.dot(before, both, preferred_element_type=F32) + run_ref[0:1, :]
    r1 = jnp.sum(jnp.where(oh1, prior, 0.0), axis=-1, keepdims=True).astype(jnp.int32)
    r2 = jnp.sum(jnp.where(oh2, prior, 0.0), axis=-1, keepdims=True).astype(jnp.int32)
    per_tok = jnp.where(lane == 0, i1 - MOE_GROUPS,
                        jnp.where(lane == 1, i2 - MOE_GROUPS,
                                  jnp.where(lane == 2, r1, jnp.where(lane == 3, r2, 0))))
    eid_ref[...] = per_tok.astype(F32).T[0:8, :].astype(jnp.int32)
    run = run_ref[...] + jnp.sum(both.astype(F32), axis=0, keepdims=True)
    run_ref[...] = run
    cnt_ref[...] = run


def _router(x, g, sc, sh, w_router, seq):
    n, d = x.shape
    tm = min(512, seq)
    return pl.pallas_call(
        _router_kernel,
        grid=(n // tm,),
        in_specs=[
            pl.BlockSpec((tm, d), lambda i: (i, 0)),
            pl.BlockSpec((1, d), lambda i: (0, 0)),
            pl.BlockSpec((None, 1, d), lambda i: (i * tm // seq, 0, 0)),
            pl.BlockSpec((None, 1, d), lambda i: (i * tm // seq, 0, 0)),
            pl.BlockSpec((d, ROUTER_LANES), lambda i: (0, 0)),
        ],
        out_specs=[
            pl.BlockSpec((tm, d // 2), lambda i: (i, 0)),
            pl.BlockSpec((8, tm), lambda i: (0, i)),
            pl.BlockSpec((tm, ROUTER_LANES), lambda i: (i, 0)),
            pl.BlockSpec((8, ROUTER_LANES), lambda i: (0, 0)),
        ],
        out_shape=[
            jax.ShapeDtypeStruct((n, d // 2), jnp.uint32),
            jax.ShapeDtypeStruct((8, n), jnp.int32),
            jax.ShapeDtypeStruct((n, ROUTER_LANES), F32),
            jax.ShapeDtypeStruct((8, ROUTER_LANES), F32),
        ],
        scratch_shapes=[pltpu.VMEM((8, ROUTER_LANES), F32)],
        compiler_params=_cparams(("arbitrary",)),
        name="moe_router",
    )(x, g.reshape(1, d), sc, sh, w_router)


def _dispatch_tables(eid, cnt):
    n = eid.shape[1]
    a = n * 2
    counts = cnt[0, MOE_GROUPS:MOE_GROUPS + MOE_EXPERTS].astype(jnp.int32)
    padded = (counts + MOE_BLOCK - 1) // MOE_BLOCK * MOE_BLOCK
    pend = jnp.cumsum(padded)
    pstart = pend - padded
    experts = jnp.arange(MOE_EXPERTS, dtype=jnp.int32)[None, :]

    def slot_of(e, rank):
        return jnp.sum(jnp.where(e[:, None] == experts, pstart[None, :], 0), axis=1) + rank

    n_blocks = (a + MOE_EXPERTS * MOE_BLOCK) // MOE_BLOCK
    bstart = jnp.arange(n_blocks, dtype=jnp.int32) * MOE_BLOCK
    block_expert = jnp.minimum(jnp.sum((pend[None, :] <= bstart[:, None]).astype(jnp.int32), axis=1),
                               MOE_EXPERTS - 1)
    nvalid = (pend[-1] // MOE_BLOCK).astype(jnp.int32).reshape(1)
    used = counts > 0
    order = jnp.cumsum(used.astype(jnp.int32)) - 1
    later = used[None, :] & (experts > experts.T)
    nxt = jnp.min(jnp.where(later, experts, MOE_EXPERTS), axis=1)
    nxt = jnp.where(nxt == MOE_EXPERTS, -1, nxt)
    return (block_expert, nvalid, slot_of(eid[0], eid[2]), slot_of(eid[1], eid[3]), pstart + counts, pend,
            order, nxt)


def _moe_kernel(be_ref, d0_ref, d1_ref, nv_ref, plo_ref, phi_ref, ord_ref, nxt_ref,
                h_hbm, wup_hbm, wdn_hbm, o_ref,
                tok_ref, xbuf, xb_ref, wup_f, wdn_f, wup_bf, wdn_bf, sem, wsem, *, tb, ff, layer):
    b = pl.program_id(0)
    nvalid = nv_ref[0]
    n_tok = d0_ref.shape[0]
    expert = be_ref[b]
    wslot = lax.rem(ord_ref[expert], 2)

    def row_copy(tok, i, slot):
        return pltpu.make_async_copy(h_hbm.at[pl.ds(tok, 1), :], xbuf.at[slot, pl.ds(i, 1), :],
                                     sem.at[slot])

    def wait_rows(slot):
        pltpu.make_async_copy(h_hbm.at[pl.ds(0, tb), :], xbuf.at[slot], sem.at[slot]).wait()

    def weight_copies(e, slot):
        return (pltpu.make_async_copy(wup_hbm.at[layer, e], wup_f.at[slot], wsem.at[0, slot]),
                pltpu.make_async_copy(wdn_hbm.at[layer, e], wdn_f.at[slot], wsem.at[1, slot]))

    @pl.when(b == 0)
    def _():
        for cp in weight_copies(expert, wslot):
            cp.start()

        def fill(i, carry):
            for u in range(MOE_FILL_UNROLL):
                t = i * MOE_FILL_UNROLL + u
                tok_ref[d0_ref[t]] = t
                tok_ref[d1_ref[t]] = t
            return carry
        lax.fori_loop(0, n_tok // MOE_FILL_UNROLL, fill, 0)

        def pad_expert(e, carry):
            def pad(s, c):
                tok_ref[s] = 0
                return c
            lax.fori_loop(plo_ref[e], phi_ref[e], pad, 0)
            return carry
        lax.fori_loop(0, MOE_EXPERTS, pad_expert, 0)

        for ahead in range(MOE_GATHER_DEPTH - 1):
            base = jnp.minimum(ahead, nvalid - 1) * tb

            def body(i, carry, base=base, ahead=ahead):
                row_copy(tok_ref[base + i], i, ahead).start()
                return carry
            lax.fori_loop(0, tb, body, 0)

    prev = be_ref[jnp.maximum(b - 1, 0)]

    @pl.when((b < nvalid) & ((b == 0) | (expert != prev)))
    def _():
        for cp in weight_copies(expert, wslot):
            cp.wait()
        wup_bf[...] = wup_f[wslot].astype(BF16)
        wdn_bf[...] = wdn_f[wslot].astype(BF16)
        nxt = nxt_ref[expert]

        @pl.when(nxt >= 0)
        def _():
            for cp in weight_copies(nxt, 1 - wslot):
                cp.start()

    @pl.when(b < nvalid)
    def _():
        slot = lax.rem(b, MOE_GATHER_DEPTH)
        wait_rows(slot)
        xp = xbuf[slot]
        half = xp.shape[1]
        xb_ref[:, :half] = lax.bitcast_convert_type(xp << 16, F32).astype(BF16)
        xb_ref[:, half:] = lax.bitcast_convert_type(xp & jnp.uint32(0xFFFF0000), F32).astype(BF16)
        ahead = MOE_GATHER_DEPTH - 1
        nslot = lax.rem(b + ahead, MOE_GATHER_DEPTH)
        base = jnp.minimum(b + ahead, nvalid - 1) * tb
        for i in range(tb):
            row_copy(tok_ref[base + i], i, nslot).start()
        hcat = jnp.dot(xb_ref[...], wup_bf[...], preferred_element_type=F32)
        act = (_silu(hcat[:, :ff]) * hcat[:, ff:]).astype(BF16)
        o_ref[...] = jnp.dot(act, wdn_bf[...], preferred_element_type=F32)

        @pl.when(b == nvalid - 1)
        def _():
            for k in range(1, MOE_GATHER_DEPTH):
                wait_rows(lax.rem(b + k, MOE_GATHER_DEPTH))

    @pl.when(b >= nvalid)
    def _():
        o_ref[...] = jnp.zeros_like(o_ref)


def _moe_experts(h2, block_expert, d0, d1, nvalid, pad_lo, pad_hi, order, nxt, w_up, w_down, layer):
    n = h2.shape[0]
    d = 2 * h2.shape[1]
    tb = MOE_BLOCK
    n_blocks = block_expert.shape[0]
    n_slots = n_blocks * tb
    ff = w_down.shape[2]
    grid_spec = pltpu.PrefetchScalarGridSpec(
        num_scalar_prefetch=8,
        grid=(n_blocks,),
        in_specs=[
            pl.BlockSpec(memory_space=pl.ANY),
            pl.BlockSpec(memory_space=pl.ANY),
            pl.BlockSpec(memory_space=pl.ANY),
        ],
        out_specs=pl.BlockSpec((tb, d), lambda b, *_: (b, 0)),
        scratch_shapes=[
            pltpu.SMEM((n_slots,), jnp.int32),
            pltpu.VMEM((MOE_GATHER_DEPTH, tb, d // 2), jnp.uint32),
            pltpu.VMEM((tb, d), BF16),
            pltpu.VMEM((2, d, 2 * ff), F32),
            pltpu.VMEM((2, ff, d), F32),
            pltpu.VMEM((d, 2 * ff), BF16),
            pltpu.VMEM((ff, d), BF16),
            pltpu.SemaphoreType.DMA((MOE_GATHER_DEPTH,)),
            pltpu.SemaphoreType.DMA((2, 2)),
        ],
    )
    return pl.pallas_call(
        functools.partial(_moe_kernel, tb=tb, ff=ff, layer=layer),
        grid_spec=grid_spec,
        out_shape=jax.ShapeDtypeStruct((n_slots, d), F32),
        compiler_params=_cparams(("arbitrary",)),
        name="moe_experts",
    )(block_expert, d0, d1, nvalid, pad_lo, pad_hi, order, nxt, h2, w_up, w_down)


def _combine_kernel(d0_ref, d1_ref, yb_hbm, x_ref, rt_ref, gate_ref, *rest, tm, tail):
    if tail == "codes":
        g_ref, sc_ref, sh_ref, wa_ref, o_ref, a_ref, ybuf, ys_ref, sem = rest
    elif tail == "final":
        g_ref, o_ref, ybuf, ys_ref, sem = rest
    else:
        o_ref, ybuf, ys_ref, sem = rest
    t = pl.program_id(0)
    nt = pl.num_programs(0)

    def row_copy(src_row, i, slot):
        return pltpu.make_async_copy(yb_hbm.at[pl.ds(src_row, 1), :], ybuf.at[slot, pl.ds(i, 1), :],
                                     sem.at[slot])

    def wait_rows(slot):
        pltpu.make_async_copy(yb_hbm.at[pl.ds(0, 2 * tm), :], ybuf.at[slot], sem.at[slot]).wait()

    def gather(blk, slot):
        base = blk * tm
        for i in range(tm):
            row_copy(d0_ref[base + i], i, slot).start()
            row_copy(d1_ref[base + i], tm + i, slot).start()

    ahead = MOE_GATHER_DEPTH - 1

    @pl.when(t == 0)
    def _():
        for k in range(ahead):
            gather(jnp.minimum(k, nt - 1), k)

    slot = lax.rem(t, MOE_GATHER_DEPTH)
    wait_rows(slot)
    rt = rt_ref[...]
    w0 = jnp.broadcast_to(rt[:, 0:1], (tm, 128))
    w1 = jnp.broadcast_to(rt[:, 1:2], (tm, 128))
    for c in range(ys_ref.shape[1] // 128):
        cols = pl.ds(c * 128, 128)
        ys_ref[:, cols] = w0 * ybuf[slot, pl.ds(0, tm), cols] + w1 * ybuf[slot, pl.ds(tm, tm), cols]
    gather(jnp.minimum(t + ahead, nt - 1), lax.rem(t + ahead, MOE_GATHER_DEPTH))
    xn = x_ref[...] + gate_ref[...] * ys_ref[...]
    if tail == "final":
        ms = jnp.mean(xn * xn, axis=-1, keepdims=True)
        o_ref[...] = xn * lax.rsqrt(ms + EPS) * g_ref[...]
    else:
        o_ref[...] = xn
    if tail == "codes":
        h = _norm_mod(xn, g_ref[...], sc_ref[...], sh_ref[...]).astype(BF16)
        a_ref[...] = jnp.dot(h, wa_ref[...], preferred_element_type=F32).astype(a_ref.dtype)

    @pl.when(t == nt - 1)
    def _():
        for k in range(1, MOE_GATHER_DEPTH):
            wait_rows(lax.rem(t + k, MOE_GATHER_DEPTH))


def _combine(yb, d0, d1, x, route, gate, seq, tail="plain", norm=None):
    n, d = x.shape
    tm = min(256, seq)
    row = lambda t, d0, d1: (t, 0)
    const = lambda t, d0, d1: (0, 0)
    per_batch = lambda t, d0, d1: (t * tm // seq, 0, 0)
    in_specs = [
        pl.BlockSpec(memory_space=pl.ANY),
        pl.BlockSpec((tm, d), row),
        pl.BlockSpec((tm, ROUTER_LANES), row),
        pl.BlockSpec((None, 1, d), per_batch),
    ]
    args = [yb, x, route, gate]
    out_specs = pl.BlockSpec((tm, d), row)
    out_shape = jax.ShapeDtypeStruct((n, d), F32)
    if tail == "codes":
        g, sc, sh, w_a = norm
        in_specs += [pl.BlockSpec((1, d), const), pl.BlockSpec((None, 1, d), per_batch),
                     pl.BlockSpec((None, 1, d), per_batch), pl.BlockSpec((d, 128), const)]
        args += [g.reshape(1, d), sc, sh, w_a]
        out_specs = [out_specs, pl.BlockSpec((tm, 128), row)]
        out_shape = [out_shape, jax.ShapeDtypeStruct((n, 128), BF16)]
    elif tail == "final":
        in_specs += [pl.BlockSpec((1, d), const)]
        args += [norm[0].reshape(1, d)]
    grid_spec = pltpu.PrefetchScalarGridSpec(
        num_scalar_prefetch=2,
        grid=(n // tm,),
        in_specs=in_specs,
        out_specs=out_specs,
        scratch_shapes=[pltpu.VMEM((MOE_GATHER_DEPTH, 2 * tm, d), F32),
                        pltpu.VMEM((tm, d), F32),
                        pltpu.SemaphoreType.DMA((MOE_GATHER_DEPTH,))],
    )
    return pl.pallas_call(
        functools.partial(_combine_kernel, tm=tm, tail=tail),
        grid_spec=grid_spec,
        out_shape=out_shape,
        compiler_params=_cparams(("arbitrary",)),
        name="moe_combine",
    )(d0, d1, *args)


def _proj_tile(nout):
    for tn in (1536, 896, 768, 512, 256, 128):
        if nout % tn == 0:
            return tn
    raise ValueError(f"no projection tile for width {nout}")


def kernel(x, c, ada_w, ada_b, norm1_g, norm2_g, na_w_in, na_w_out, na_rpb, gla_w_in, gla_w_gate_up, gla_b_gate, gla_gn_g, gla_w_out, moe_w_router_group, moe_w_router_expert, moe_w_up, moe_w_down, final_g):
    batch, seq, d = x.shape
    depth = ada_w.shape[0]
    n = batch * seq
    assert seq % (GRID_W * NA_ROWS_PER_STEP) == 0 and d % 1024 == 0
    dk = d // 2

    mod = _adaln(c, ada_w, ada_b)
    xf = x.reshape(n, d)
    i_na = 0
    i_gla = 0
    for i in range(depth):
        sh1, sc1, gt1, sh2, sc2, gt2 = [m.reshape(batch, 1, d) for m in jnp.split(mod[i], 6, axis=-1)]
        if i % 2 == 0:
            w_in = na_w_in[i_na].astype(BF16)
            colscale = jnp.concatenate([jnp.full((d,), NA_HEAD_DIM ** -0.5, F32), jnp.ones((2 * d,), F32)])
            qkv = _norm_mod_matmul(xf, norm1_g[i], sc1, sh1, w_in, colscale, seq, _proj_tile(3 * d))
            bias_tab = _na_bias_table(na_rpb[i_na])
            y = _na_attention(qkv, bias_tab, batch, seq, d)
            w_out = na_w_out[i_na].astype(BF16)
            i_na += 1
        else:
            hk = dk // GLA_HEADS
            w_in = gla_w_in[i_gla][:, :3 * d].astype(BF16)
            colscale = jnp.concatenate([jnp.full((dk,), hk ** -0.5, F32), jnp.ones((3 * d - dk,), F32)])
            proj = _norm_mod_matmul(xf, norm1_g[i], sc1, sh1, w_in, colscale, seq, _proj_tile(3 * d))
            wg_pad = jnp.zeros((2, 128, dk), F32)
            wg_pad = wg_pad.at[0, :GLA_GATE_RANK].set(gla_w_gate_up[i_gla, 0])
            wg_pad = wg_pad.at[1, GLA_GATE_RANK:2 * GLA_GATE_RANK].set(gla_w_gate_up[i_gla, 1])
            y = _gla(proj, a_low, wg_pad.astype(BF16), gla_b_gate[i_gla], gla_gn_g[i_gla], batch, seq, d)
            w_out = gla_w_out[i_gla].astype(BF16)
            i_gla += 1
        xf = _matmul_residual(y, w_out, xf, gt1, seq)

        w_router = jnp.concatenate(
            [moe_w_router_group[i], moe_w_router_expert[i],
             jnp.zeros((d, ROUTER_LANES - MOE_GROUPS - MOE_EXPERTS), F32)], axis=1).astype(BF16)
        h2, eid, route, cnt = _router(xf, norm2_g[i], sc2, sh2, w_router, seq)
        block_expert, nvalid, d0, d1, pad_lo, pad_hi, order, nxt = _dispatch_tables(eid, cnt)
        yb = _moe_experts(h2, block_expert, d0, d1, nvalid, pad_lo, pad_hi, order, nxt,
                          moe_w_up, moe_w_down, i)
        if i == depth - 1:
            xf = _combine(yb, d0, d1, xf, route, gt2, seq, "final", (final_g,))
        elif (i + 1) % 2 == 1:
            nsh1, nsc1 = [m.reshape(batch, 1, d) for m in jnp.split(mod[i + 1], 6, axis=-1)[:2]]
            w_a = jnp.concatenate([gla_w_in[i_gla][:, 3 * d:],
                                   jnp.zeros((d, 128 - 2 * GLA_GATE_RANK), F32)], axis=1).astype(BF16)
            xf, a_low = _combine(yb, d0, d1, xf, route, gt2, seq, "codes",
                                 (norm1_g[i + 1], nsc1, nsh1, w_a))
        else:
            xf = _combine(yb, d0, d1, xf, route, gt2, seq)

    return xf.reshape(batch, seq, d)
```

```python
import functools

import jax
import jax.numpy as jnp
import numpy as np
from jax import lax
from jax.experimental import pallas as pl
from jax.experimental.pallas import tpu as pltpu

F32 = jnp.float32
BF16 = jnp.bfloat16

EPS = 1e-6
GRID_W = 64
NA_HEAD_DIM = 32
NA_WIN_ROWS = 8
NA_WIN_COLS = 16
NA_HEADS_PER_CHUNK = 4
NA_QBLOCKS = ((0, 24, 0), (24, 16, 16), (40, 24, 32))
NA_KEY_COLS = 32
NA_ROWS_PER_STEP = 64
NA_GROUP_ROWS = 64
MASK_VALUE = -1e30

GLA_HEADS = 4
GLA_GATE_RANK = 16
GLA_GATE_NORM = 16.0
GLA_CHUNK = 64
GLA_CHUNKS_PER_STEP = 16

MOE_GROUPS = 4
MOE_EXPERTS_PER_GROUP = 8
MOE_EXPERTS = MOE_GROUPS * MOE_EXPERTS_PER_GROUP
MOE_BLOCK = 256
MOE_GATHER_DEPTH = 3
MOE_FILL_UNROLL = 16
ROUTER_LANES = 128

VMEM_LIMIT = 56 * 1024 * 1024


def _cparams(sem):
    return pltpu.CompilerParams(dimension_semantics=sem, vmem_limit_bytes=VMEM_LIMIT)


def _silu(v):
    return v * jax.nn.sigmoid(v)


def _adaln_kernel(c_ref, w_ref, b_ref, o_ref):
    cond = _silu(c_ref[...]).astype(BF16)
    w = w_ref[...].astype(BF16)
    o_ref[...] = jnp.dot(cond, w, preferred_element_type=F32) + b_ref[...]


def _adaln(c, ada_w, ada_b):
    depth, d, n6 = ada_w.shape
    b = c.shape[0]
    rows = 16
    cpad = jnp.zeros((rows, d), F32).at[:b].set(c)
    tn = 1024
    out = pl.pallas_call(
        _adaln_kernel,
        grid=(depth, n6 // tn),
        in_specs=[
            pl.BlockSpec((rows, d), lambda i, j: (0, 0)),
            pl.BlockSpec((None, d, tn), lambda i, j: (i, 0, j)),
            pl.BlockSpec((None, 1, tn), lambda i, j: (i, 0, j)),
        ],
        out_specs=pl.BlockSpec((None, rows, tn), lambda i, j: (i, 0, j)),
        out_shape=jax.ShapeDtypeStruct((depth, rows, n6), F32),
        compiler_params=_cparams(("arbitrary", "arbitrary")),
        name="adaln",
    )(cpad, ada_w, ada_b.reshape(depth, 1, n6))
    return out[:, :b]


def _norm_mod(x, g, sc, sh):
    ms = jnp.mean(x * x, axis=-1, keepdims=True)
    y = x * lax.rsqrt(ms + EPS) * g
    return y * (1.0 + sc) + sh


def _nmm_kernel(x_ref, g_ref, sc_ref, sh_ref, w_ref, cs_ref, o_ref, h_ref):
    @pl.when(pl.program_id(1) == 0)
    def _():
        h_ref[...] = _norm_mod(x_ref[...], g_ref[...], sc_ref[...], sh_ref[...]).astype(BF16)

    acc = jnp.dot(h_ref[...], w_ref[...], preferred_element_type=F32)
    o_ref[...] = (acc * cs_ref[...]).astype(o_ref.dtype)


def _norm_mod_matmul(x, g, sc, sh, w, colscale, seq, tn):
    n, d = x.shape
    nout = w.shape[1]
    tm = min(1024, seq)
    return pl.pallas_call(
        _nmm_kernel,
        grid=(n // tm, nout // tn),
        in_specs=[
            pl.BlockSpec((tm, d), lambda i, j: (i, 0)),
            pl.BlockSpec((1, d), lambda i, j: (0, 0)),
            pl.BlockSpec((None, 1, d), lambda i, j: (i * tm // seq, 0, 0)),
            pl.BlockSpec((None, 1, d), lambda i, j: (i * tm // seq, 0, 0)),
            pl.BlockSpec((d, tn), lambda i, j: (0, j)),
            pl.BlockSpec((1, tn), lambda i, j: (0, j)),
        ],
        out_specs=pl.BlockSpec((tm, tn), lambda i, j: (i, j)),
        out_shape=jax.ShapeDtypeStruct((n, nout), BF16),
        scratch_shapes=[pltpu.VMEM((tm, d), BF16)],
        compiler_params=_cparams(("arbitrary", "arbitrary")),
        name="norm_mod_matmul",
    )(x, g.reshape(1, d), sc, sh, w, colscale.reshape(1, nout))


def _mmres_kernel(a_ref, w_ref, res_ref, gate_ref, o_ref):
    acc = jnp.dot(a_ref[...], w_ref[...], preferred_element_type=F32)
    o_ref[...] = res_ref[...] + gate_ref[...] * acc


def _matmul_residual(a, w, res, gate, seq):
    n, k = a.shape
    d = w.shape[1]
    tm = min(512, seq)
    return pl.pallas_call(
        _mmres_kernel,
        grid=(n // tm,),
        in_specs=[
            pl.BlockSpec((tm, k), lambda i: (i, 0)),
            pl.BlockSpec((k, d), lambda i: (0, 0)),
            pl.BlockSpec((tm, d), lambda i: (i, 0)),
            pl.BlockSpec((None, 1, d), lambda i: (i * tm // seq, 0, 0)),
        ],
        out_specs=pl.BlockSpec((tm, d), lambda i: (i, 0)),
        out_shape=jax.ShapeDtypeStruct((n, d), F32),
        compiler_params=_cparams(("arbitrary",)),
        name="matmul_residual",
    )(a, w, res, gate)


def _na_bias_table(rpb):
    heads = rpb.shape[0]
    chunks = heads // NA_HEADS_PER_CHUNK
    wr, wc = NA_WIN_ROWS, NA_WIN_COLS
    rpb = rpb.astype(F32)
    a = jnp.stack([rpb[:, wr - 1 - di:2 * wr - 1 - di, :] for di in range(wr)], axis=1)
    blocks = []
    for qs, nq, ks in NA_QBLOCKS:
        cols = []
        for q in range(qs, qs + nq):
            cstart = min(max(q - wc // 2, 0), GRID_W - wc)
            first = cstart - q + wc - 1
            off = cstart - ks
            cols.append(jnp.pad(a[..., first:first + wc],
                                ((0, 0), (0, 0), (0, 0), (off, NA_KEY_COLS - wc - off)),
                                constant_values=MASK_VALUE))
        t = jnp.stack(cols, axis=3)
        t = t.reshape(chunks, NA_HEADS_PER_CHUNK, wr, wr, nq, NA_KEY_COLS)
        t = t.transpose(0, 2, 1, 4, 3, 5)
        blocks.append(t.reshape(chunks, wr, NA_HEADS_PER_CHUNK * nq, wr * NA_KEY_COLS))
    return jnp.concatenate(blocks, axis=2)


def _na_kernel(q_ref, k_ref, v_ref, bias_ref, o_ref, l_ref, m_ref, p_ref, *, n_rows):
    rblk = pl.program_id(2)
    lane = lax.broadcasted_iota(jnp.int32, (1, 128), 1)
    hmask = [(lane // NA_HEAD_DIM == hp).astype(F32) for hp in range(NA_HEADS_PER_CHUNK)]
    win_tokens = NA_WIN_ROWS * GRID_W
    nkeys = NA_WIN_ROWS * NA_KEY_COLS
    ones = jnp.ones((nkeys, 128), BF16)

    def window(rr):
        r = rblk * NA_ROWS_PER_STEP + rr
        rs = jnp.clip(r - NA_WIN_ROWS // 2, 0, n_rows - NA_WIN_ROWS)
        return pl.multiple_of(rs * GRID_W, GRID_W), r - rs

    def key_block(win, ks):
        return jnp.concatenate([win[i * GRID_W + ks:i * GRID_W + ks + NA_KEY_COLS]
                                for i in range(NA_WIN_ROWS)], axis=0)

    def scores_and_max(grp):
        for j in range(NA_GROUP_ROWS):
            rr = grp * NA_GROUP_ROWS + j
            w0, di = window(rr)
            q0 = pl.multiple_of(rr * GRID_W, GRID_W)
            qrow = q_ref[pl.ds(q0, GRID_W), :].astype(F32)
            kwin = k_ref[pl.ds(w0, win_tokens), :]
            row0 = 0
            for qs, nq, ks in NA_QBLOCKS:
                nst = NA_HEADS_PER_CHUNK * nq
                qb = qrow[qs:qs + nq]
                qstack = jnp.concatenate([qb * hmask[hp] for hp in range(NA_HEADS_PER_CHUNK)],
                                         axis=0).astype(BF16)
                logits = lax.dot_general(qstack, key_block(kwin, ks), (((1,), (1,)), ((), ())),
                                         preferred_element_type=F32)
                l_ref[rr, pl.ds(row0, nst), :] = logits + bias_ref[di, pl.ds(row0, nst), :]
                row0 += nst
        for j in range(NA_GROUP_ROWS):
            rr = grp * NA_GROUP_ROWS + j
            m = jnp.max(l_ref[rr], axis=-1, keepdims=True)
            m_ref[rr] = jnp.broadcast_to(m, m_ref.shape[1:])

    def softmax_and_values(grp):
        for j in range(NA_GROUP_ROWS):
            rr = grp * NA_GROUP_ROWS + j
            m = m_ref[rr]
            p_ref[rr] = jnp.exp(l_ref[rr] - jnp.concatenate([m, m], axis=1)).astype(BF16)
        for j in range(NA_GROUP_ROWS):
            rr = grp * NA_GROUP_ROWS + j
            w0, _ = window(rr)
            vwin = v_ref[pl.ds(w0, win_tokens), :]
            outs = []
            row0 = 0
            for qs, nq, ks in NA_QBLOCKS:
                nst = NA_HEADS_PER_CHUNK * nq
                v1 = jnp.concatenate([key_block(vwin, ks), ones], axis=1)
                acc = jnp.dot(p_ref[rr, pl.ds(row0, nst), :], v1, preferred_element_type=F32)
                num = acc[0:nq, :128] * hmask[0]
                den = acc[0:nq, 128:] * hmask[0]
                for hp in range(1, NA_HEADS_PER_CHUNK):
                    num = num + acc[hp * nq:(hp + 1) * nq, :128] * hmask[hp]
                    den = den + acc[hp * nq:(hp + 1) * nq, 128:] * hmask[hp]
                outs.append(num * (1.0 / den))
                row0 += nst
            orow = jnp.concatenate(outs, axis=0)
            o_ref[pl.ds(pl.multiple_of(rr * GRID_W, GRID_W), GRID_W), :] = orow.astype(o_ref.dtype)

    n_groups = NA_ROWS_PER_STEP // NA_GROUP_ROWS
    scores_and_max(0)

    def body(g, carry):
        scores_and_max(g + 1)
        softmax_and_values(g)
        return carry

    lax.fori_loop(0, n_groups - 1, body, 0)
    softmax_and_values(n_groups - 1)


def _na_attention(qkv, bias_tab, batch, seq, d):
    n = qkv.shape[0]
    n_rows = seq // GRID_W
    chunks = d // 128
    step_tokens = NA_ROWS_PER_STEP * GRID_W
    rsteps = n_rows // NA_ROWS_PER_STEP
    nrow = bias_tab.shape[2]
    nkey = bias_tab.shape[3]
    return pl.pallas_call(
        functools.partial(_na_kernel, n_rows=n_rows),
        grid=(batch, chunks, rsteps),
        in_specs=[
            pl.BlockSpec((step_tokens, 128), lambda b, c, r: (b * rsteps + r, c)),
            pl.BlockSpec((seq, 128), lambda b, c, r: (b, chunks + c)),
            pl.BlockSpec((seq, 128), lambda b, c, r: (b, 2 * chunks + c)),
            pl.BlockSpec((None, NA_WIN_ROWS, nrow, nkey), lambda b, c, r: (c, 0, 0, 0)),
        ],
        out_specs=pl.BlockSpec((step_tokens, 128), lambda b, c, r: (b * rsteps + r, c)),
        out_shape=jax.ShapeDtypeStruct((n, d), BF16),
        scratch_shapes=[
            pltpu.VMEM((NA_ROWS_PER_STEP, nrow, nkey), F32),
            pltpu.VMEM((NA_ROWS_PER_STEP, nrow, 128), F32),
            pltpu.VMEM((NA_ROWS_PER_STEP, nrow, nkey), BF16),
        ],
        compiler_params=_cparams(("arbitrary", "arbitrary", "arbitrary")),
        name="na_attention",
    )(qkv, qkv, qkv, bias_tab)


def _log_sigmoid(v):
    return jnp.minimum(v, 0.0) - jnp.log(1.0 + jnp.exp(-jnp.abs(v)))


def _gla_kernel(q_ref, k_ref, v_ref, r_ref, a_ref, wg_ref, bg_ref, gn_ref, o_ref,
                st_ref, of_ref, cum_ref, qe_ref, ke_ref, kdt_ref, att_ref, av_ref, u_ref, dec_ref, os_ref,
                *, cb, nb):
    p = pl.program_id(2)
    i = pl.program_id(3)
    fwd = p == 0
    step_rows = cb * GLA_CHUNK

    @pl.when(i == 0)
    def _():
        st_ref[...] = jnp.zeros_like(st_ref)

    blk = jnp.where(fwd, i, nb - 1 - i)
    tr = lax.broadcasted_iota(jnp.int32, (GLA_CHUNK, GLA_CHUNK), 0)
    tc = lax.broadcasted_iota(jnp.int32, (GLA_CHUNK, GLA_CHUNK), 1)
    tmask = (tr - tc) * (1 - 2 * p) >= 0
    tmat = tmask.astype(BF16)
    nt = (((1,), (1,)), ((), ()))

    gl = jnp.dot(a_ref[...], wg_ref[...], preferred_element_type=F32) + bg_ref[...]
    g = _log_sigmoid(gl) * (1.0 / GLA_GATE_NORM)
    g1 = g.astype(BF16)
    e1 = g - g1.astype(F32)
    g2 = e1.astype(BF16)
    g3 = (e1 - g2.astype(F32)).astype(BF16)
    for c in range(cb):
        rows = slice(c * GLA_CHUNK, (c + 1) * GLA_CHUNK)
        cum_ref[rows, :] = (jnp.dot(tmat, g1[rows], preferred_element_type=F32)
                            + jnp.dot(tmat, g2[rows], preferred_element_type=F32)
                            + jnp.dot(tmat, g3[rows], preferred_element_type=F32))
    for c in range(cb):
        rows = slice(c * GLA_CHUNK, (c + 1) * GLA_CHUNK)
        cum = cum_ref[rows, :]
        last = jnp.where(fwd, cum[GLA_CHUNK - 1:GLA_CHUNK], cum[0:1])
        q = q_ref[rows, :].astype(F32)
        k = k_ref[rows, :].astype(F32)
        qe_ref[c] = (q * jnp.exp(cum)).astype(BF16)
        ke_ref[c] = (k * jnp.exp(-cum)).astype(BF16)
        kdt_ref[c] = (k * jnp.exp(last - cum)).T.astype(BF16)
        dcol = jnp.broadcast_to(jnp.exp(last), (8, last.shape[1])).T
        dec_ref[c] = jnp.broadcast_to(dcol[:, 0:1], dec_ref.shape[1:])
    for c in range(cb):
        att = lax.dot_general(qe_ref[c], ke_ref[c], nt, preferred_element_type=F32)
        att_ref[c] = jnp.where(tmask, att, 0.0).astype(BF16)
    for c in range(cb):
        rows = slice(c * GLA_CHUNK, (c + 1) * GLA_CHUNK)
        v = v_ref[rows, :]
        av_ref[c] = jnp.dot(att_ref[c], v, preferred_element_type=F32)
        u_ref[c] = jnp.dot(kdt_ref[c], v, preferred_element_type=F32)

    lane_tiles = st_ref.shape[1] // 128
    for cc in range(cb):
        c = jnp.where(fwd, cc, cb - 1 - cc)
        st = st_ref[...]
        o = av_ref[c] + jnp.dot(qe_ref[c], st.astype(BF16), preferred_element_type=F32)
        dec = dec_ref[c]
        st_ref[...] = st * jnp.concatenate([dec] * lane_tiles, axis=1) + u_ref[c]
        os_ref[pl.ds(pl.multiple_of(c * GLA_CHUNK, GLA_CHUNK), GLA_CHUNK), :] = o

    g0 = pl.multiple_of(blk * step_rows, step_rows)

    @pl.when(fwd)
    def _():
        of_ref[pl.ds(g0, step_rows), :] = os_ref[...]

    @pl.when(jnp.logical_not(fwd))
    def _():
        ot = of_ref[pl.ds(g0, step_rows), :] + os_ref[...]
        ms = jnp.mean(ot * ot, axis=-1, keepdims=True)
        y = ot * lax.rsqrt(ms + EPS) * gn_ref[...]
        y = y * _silu(r_ref[...].astype(F32))
        o_ref[...] = y.astype(o_ref.dtype)


def _gla(proj, a_low, wg_pad, b_gate, gn_g, batch, seq, d):
    n = proj.shape[0]
    hk = d // 2 // GLA_HEADS
    hv = d // GLA_HEADS
    cb = min(GLA_CHUNKS_PER_STEP, seq // GLA_CHUNK)
    step_rows = cb * GLA_CHUNK
    nb = seq // step_rows

    def rowblk(b, p, i):
        return b * nb + jnp.where(p == 0, i, nb - 1 - i)

    def outblk(b, p, i):
        return b * nb + jnp.where(p == 0, nb - 1, nb - 1 - i)

    return pl.pallas_call(
        functools.partial(_gla_kernel, cb=cb, nb=nb),
        grid=(batch, GLA_HEADS, 2, nb),
        in_specs=[
            pl.BlockSpec((step_rows, hk), lambda b, h, p, i: (rowblk(b, p, i), h)),
            pl.BlockSpec((step_rows, hk), lambda b, h, p, i: (rowblk(b, p, i), GLA_HEADS + h)),
            pl.BlockSpec((step_rows, hv), lambda b, h, p, i: (rowblk(b, p, i), GLA_HEADS + h)),
            pl.BlockSpec((step_rows, hv), lambda b, h, p, i: (rowblk(b, p, i), 2 * GLA_HEADS + h)),
            pl.BlockSpec((step_rows, 128), lambda b, h, p, i: (rowblk(b, p, i), 0)),
            pl.BlockSpec((None, 128, hk), lambda b, h, p, i: (p, 0, h)),
            pl.BlockSpec((None, 1, hk), lambda b, h, p, i: (p, 0, h)),
            pl.BlockSpec((1, hv), lambda b, h, p, i: (0, h)),
        ],
        out_specs=pl.BlockSpec((step_rows, hv), lambda b, h, p, i: (outblk(b, p, i), h)),
        out_shape=jax.ShapeDtypeStruct((n, d), BF16),
        scratch_shapes=[
            pltpu.VMEM((hk, hv), F32),
            pltpu.VMEM((seq, hv), F32),
            pltpu.VMEM((step_rows, hk), F32),
            pltpu.VMEM((cb, GLA_CHUNK, hk), BF16),
            pltpu.VMEM((cb, GLA_CHUNK, hk), BF16),
            pltpu.VMEM((cb, hk, GLA_CHUNK), BF16),
            pltpu.VMEM((cb, GLA_CHUNK, GLA_CHUNK), BF16),
            pltpu.VMEM((cb, GLA_CHUNK, hv), F32),
            pltpu.VMEM((cb, hk, hv), F32),
            pltpu.VMEM((cb, hk, 128), F32),
            pltpu.VMEM((step_rows, hv), F32),
        ],
        compiler_params=_cparams(("arbitrary", "arbitrary", "arbitrary", "arbitrary")),
        name="gla",
    )(proj, proj, proj, proj, a_low, wg_pad, b_gate.reshape(2, 1, -1), gn_g.reshape(1, d))


def _router_kernel(x_ref, g_ref, sc_ref, sh_ref, w_ref, h_ref, eid_ref, rt_ref, cnt_ref, run_ref):
    @pl.when(pl.program_id(0) == 0)
    def _():
        run_ref[...] = jnp.zeros_like(run_ref)

    h = _norm_mod(x_ref[...], g_ref[...], sc_ref[...], sh_ref[...])
    hb = h.astype(BF16)
    half = h.shape[1] // 2
    bits = lax.bitcast_convert_type(hb.astype(F32), jnp.uint32)
    h_ref[...] = (bits[:, :half] >> 16) | (bits[:, half:] & jnp.uint32(0xFFFF0000))
    logits = jnp.dot(hb, w_ref[...], preferred_element_type=F32)
    lane = lax.broadcasted_iota(jnp.int32, logits.shape, 1)
    gl = jnp.where(lane < MOE_GROUPS, logits, MASK_VALUE)
    gmax = jnp.max(gl, axis=-1, keepdims=True)
    gidx = jnp.min(jnp.where(gl == gmax, lane, ROUTER_LANES), axis=-1, keepdims=True)
    g_w = 1.0 / jnp.sum(jnp.exp(gl - gmax), axis=-1, keepdims=True)
    e_lane = lane - MOE_GROUPS
    in_group = (e_lane >= 0) & (e_lane < MOE_EXPERTS) & ((e_lane >> 3) == gidx)
    el = jnp.where(in_group, logits, MASK_VALUE)
    m1 = jnp.max(el, axis=-1, keepdims=True)
    i1 = jnp.min(jnp.where(el == m1, lane, ROUTER_LANES), axis=-1, keepdims=True)
    el2 = jnp.where(lane == i1, MASK_VALUE, el)
    m2 = jnp.max(el2, axis=-1, keepdims=True)
    i2 = jnp.min(jnp.where(el2 == m2, lane, ROUTER_LANES), axis=-1, keepdims=True)
    t = jnp.exp(m2 - m1)
    w1 = g_w / (1.0 + t)
    w2 = w1 * t
    rt_ref[...] = jnp.where(lane == 0, w1, jnp.where(lane == 1, w2, 0.0))
    tm = logits.shape[0]
    oh1 = lane == i1
    oh2 = lane == i2
    both = (oh1 | oh2).astype(BF16)
    tr = lax.broadcasted_iota(jnp.int32, (tm, tm), 0)
    tc = lax.broadcasted_iota(jnp.int32, (tm, tm), 1)
    before = (tc < tr).astype(BF16)
    prior = jnp.dot(before, both, preferred_element_type=F32) + run_ref[0:1, :]
    r1 = jnp.sum(jnp.where(oh1, prior, 0.0), axis=-1, keepdims=True).astype(jnp.int32)
    r2 = jnp.sum(jnp.where(oh2, prior, 0.0), axis=-1, keepdims=True).astype(jnp.int32)
    per_tok = jnp.where(lane == 0, i1 - MOE_GROUPS,
                        jnp.where(lane == 1, i2 - MOE_GROUPS,
                                  jnp.where(lane == 2, r1, jnp.where(lane == 3, r2, 0))))
    eid_ref[...] = per_tok.astype(F32).T[0:8, :].astype(jnp.int32)
    run = run_ref[...] + jnp.sum(both.astype(F32), axis=0, keepdims=True)
    run_ref[...] = run
    cnt_ref[...] = run


def _router(x, g, sc, sh, w_router, seq):
    n, d = x.shape
    tm = min(512, seq)
    return pl.pallas_call(
        _router_kernel,
        grid=(n // tm,),
        in_specs=[
            pl.BlockSpec((tm, d), lambda i: (i, 0)),
            pl.BlockSpec((1, d), lambda i: (0, 0)),
            pl.BlockSpec((None, 1, d), lambda i: (i * tm // seq, 0, 0)),
            pl.BlockSpec((None, 1, d), lambda i: (i * tm // seq, 0, 0)),
            pl.BlockSpec((d, ROUTER_LANES), lambda i: (0, 0)),
        ],
        out_specs=[
            pl.BlockSpec((tm, d // 2), lambda i: (i, 0)),
            pl.BlockSpec((8, tm), lambda i: (0, i)),
            pl.BlockSpec((tm, ROUTER_LANES), lambda i: (i, 0)),
            pl.BlockSpec((8, ROUTER_LANES), lambda i: (0, 0)),
        ],
        out_shape=[
            jax.ShapeDtypeStruct((n, d // 2), jnp.uint32),
            jax.ShapeDtypeStruct((8, n), jnp.int32),
            jax.ShapeDtypeStruct((n, ROUTER_LANES), F32),
            jax.ShapeDtypeStruct((8, ROUTER_LANES), F32),
        ],
        scratch_shapes=[pltpu.VMEM((8, ROUTER_LANES), F32)],
        compiler_params=_cparams(("arbitrary",)),
        name="moe_router",
    )(x, g.reshape(1, d), sc, sh, w_router)


def _dispatch_tables(eid, cnt):
    n = eid.shape[1]
    a = n * 2
    counts = cnt[0, MOE_GROUPS:MOE_GROUPS + MOE_EXPERTS].astype(jnp.int32)
    padded = (counts + MOE_BLOCK - 1) // MOE_BLOCK * MOE_BLOCK
    pend = jnp.cumsum(padded)
    pstart = pend - padded
    experts = jnp.arange(MOE_EXPERTS, dtype=jnp.int32)[None, :]

    def slot_of(e, rank):
        return jnp.sum(jnp.where(e[:, None] == experts, pstart[None, :], 0), axis=1) + rank

    n_blocks = (a + MOE_EXPERTS * MOE_BLOCK) // MOE_BLOCK
    bstart = jnp.arange(n_blocks, dtype=jnp.int32) * MOE_BLOCK
    block_expert = jnp.minimum(jnp.sum((pend[None, :] <= bstart[:, None]).astype(jnp.int32), axis=1),
                               MOE_EXPERTS - 1)
    nvalid = (pend[-1] // MOE_BLOCK).astype(jnp.int32).reshape(1)
    used = counts > 0
    order = jnp.cumsum(used.astype(jnp.int32)) - 1
    later = used[None, :] & (experts > experts.T)
    nxt = jnp.min(jnp.where(later, experts, MOE_EXPERTS), axis=1)
    nxt = jnp.where(nxt == MOE_EXPERTS, -1, nxt)
    return (block_expert, nvalid, slot_of(eid[0], eid[2]), slot_of(eid[1], eid[3]), pstart + counts, pend,
            order, nxt)


def _moe_kernel(be_ref, d0_ref, d1_ref, nv_ref, plo_ref, phi_ref, ord_ref, nxt_ref,
                h_hbm, wup_hbm, wdn_hbm, o_ref,
                tok_ref, xbuf, xb_ref, wup_f, wdn_f, wup_bf, wdn_bf, sem, wsem, *, tb, ff, layer):
    b = pl.program_id(0)
    nvalid = nv_ref[0]
    n_tok = d0_ref.shape[0]
    expert = be_ref[b]
    wslot = lax.rem(ord_ref[expert], 2)

    def row_copy(tok, i, slot):
        return pltpu.make_async_copy(h_hbm.at[pl.ds(tok, 1), :], xbuf.at[slot, pl.ds(i, 1), :],
                                     sem.at[slot])

    def wait_rows(slot):
        pltpu.make_async_copy(h_hbm.at[pl.ds(0, tb), :], xbuf.at[slot], sem.at[slot]).wait()

    def weight_copies(e, slot):
        return (pltpu.make_async_copy(wup_hbm.at[layer, e], wup_f.at[slot], wsem.at[0, slot]),
                pltpu.make_async_copy(wdn_hbm.at[layer, e], wdn_f.at[slot], wsem.at[1, slot]))

    @pl.when(b == 0)
    def _():
        for cp in weight_copies(expert, wslot):
            cp.start()

        def fill(i, carry):
            for u in range(MOE_FILL_UNROLL):
                t = i * MOE_FILL_UNROLL + u
                tok_ref[d0_ref[t]] = t
                tok_ref[d1_ref[t]] = t
            return carry
        lax.fori_loop(0, n_tok // MOE_FILL_UNROLL, fill, 0)

        def pad_expert(e, carry):
            def pad(s, c):
                tok_ref[s] = 0
                return c
            lax.fori_loop(plo_ref[e], phi_ref[e], pad, 0)
            return carry
        lax.fori_loop(0, MOE_EXPERTS, pad_expert, 0)

        for ahead in range(MOE_GATHER_DEPTH - 1):
            base = jnp.minimum(ahead, nvalid - 1) * tb

            def body(i, carry, base=base, ahead=ahead):
                row_copy(tok_ref[base + i], i, ahead).start()
                return carry
            lax.fori_loop(0, tb, body, 0)

    prev = be_ref[jnp.maximum(b - 1, 0)]

    @pl.when((b < nvalid) & ((b == 0) | (expert != prev)))
    def _():
        for cp in weight_copies(expert, wslot):
            cp.wait()
        wup_bf[...] = wup_f[wslot].astype(BF16)
        wdn_bf[...] = wdn_f[wslot].astype(BF16)
        nxt = nxt_ref[expert]

        @pl.when(nxt >= 0)
        def _():
            for cp in weight_copies(nxt, 1 - wslot):
                cp.start()

    @pl.when(b < nvalid)
    def _():
        slot = lax.rem(b, MOE_GATHER_DEPTH)
        wait_rows(slot)
        xp = xbuf[slot]
        half = xp.shape[1]
        xb_ref[:, :half] = lax.bitcast_convert_type(xp << 16, F32).astype(BF16)
        xb_ref[:, half:] = lax.bitcast_convert_type(xp & jnp.uint32(0xFFFF0000), F32).astype(BF16)
        ahead = MOE_GATHER_DEPTH - 1
        nslot = lax.rem(b + ahead, MOE_GATHER_DEPTH)
        base = jnp.minimum(b + ahead, nvalid - 1) * tb
        for i in range(tb):
            row_copy(tok_ref[base + i], i, nslot).start()
        hcat = jnp.dot(xb_ref[...], wup_bf[...], preferred_element_type=F32)
        act = (_silu(hcat[:, :ff]) * hcat[:, ff:]).astype(BF16)
        o_ref[...] = jnp.dot(act, wdn_bf[...], preferred_element_type=F32)

        @pl.when(b == nvalid - 1)
        def _():
            for k in range(1, MOE_GATHER_DEPTH):
                wait_rows(lax.rem(b + k, MOE_GATHER_DEPTH))

    @pl.when(b >= nvalid)
    def _():
        o_ref[...] = jnp.zeros_like(o_ref)


def _moe_experts(h2, block_expert, d0, d1, nvalid, pad_lo, pad_hi, order, nxt, w_up, w_down, layer):
    n = h2.shape[0]
    d = 2 * h2.shape[1]
    tb = MOE_BLOCK
    n_blocks = block_expert.shape[0]
    n_slots = n_blocks * tb
    ff = w_down.shape[2]
    grid_spec = pltpu.PrefetchScalarGridSpec(
        num_scalar_prefetch=8,
        grid=(n_blocks,),
        in_specs=[
            pl.BlockSpec(memory_space=pl.ANY),
            pl.BlockSpec(memory_space=pl.ANY),
            pl.BlockSpec(memory_space=pl.ANY),
        ],
        out_specs=pl.BlockSpec((tb, d), lambda b, *_: (b, 0)),
        scratch_shapes=[
            pltpu.SMEM((n_slots,), jnp.int32),
            pltpu.VMEM((MOE_GATHER_DEPTH, tb, d // 2), jnp.uint32),
            pltpu.VMEM((tb, d), BF16),
            pltpu.VMEM((2, d, 2 * ff), F32),
            pltpu.VMEM((2, ff, d), F32),
            pltpu.VMEM((d, 2 * ff), BF16),
            pltpu.VMEM((ff, d), BF16),
            pltpu.SemaphoreType.DMA((MOE_GATHER_DEPTH,)),
            pltpu.SemaphoreType.DMA((2, 2)),
        ],
    )
    return pl.pallas_call(
        functools.partial(_moe_kernel, tb=tb, ff=ff, layer=layer),
        grid_spec=grid_spec,
        out_shape=jax.ShapeDtypeStruct((n_slots, d), F32),
        compiler_params=_cparams(("arbitrary",)),
        name="moe_experts",
    )(block_expert, d0, d1, nvalid, pad_lo, pad_hi, order, nxt, h2, w_up, w_down)


def _combine_kernel(d0_ref, d1_ref, yb_hbm, x_ref, rt_ref, gate_ref, *rest, tm, tail):
    if tail == "codes":
        g_ref, sc_ref, sh_ref, wa_ref, o_ref, a_ref, ybuf, ys_ref, sem = rest
    elif tail == "final":
        g_ref, o_ref, ybuf, ys_ref, sem = rest
    else:
        o_ref, ybuf, ys_ref, sem = rest
    t = pl.program_id(0)
    nt = pl.num_programs(0)

    def row_copy(src_row, i, slot):
        return pltpu.make_async_copy(yb_hbm.at[pl.ds(src_row, 1), :], ybuf.at[slot, pl.ds(i, 1), :],
                                     sem.at[slot])

    def wait_rows(slot):
        pltpu.make_async_copy(yb_hbm.at[pl.ds(0, 2 * tm), :], ybuf.at[slot], sem.at[slot]).wait()

    def gather(blk, slot):
        base = blk * tm
        for i in range(tm):
            row_copy(d0_ref[base + i], i, slot).start()
            row_copy(d1_ref[base + i], tm + i, slot).start()

    ahead = MOE_GATHER_DEPTH - 1

    @pl.when(t == 0)
    def _():
        for k in range(ahead):
            gather(jnp.minimum(k, nt - 1), k)

    slot = lax.rem(t, MOE_GATHER_DEPTH)
    wait_rows(slot)
    rt = rt_ref[...]
    w0 = jnp.broadcast_to(rt[:, 0:1], (tm, 128))
    w1 = jnp.broadcast_to(rt[:, 1:2], (tm, 128))
    for c in range(ys_ref.shape[1] // 128):
        cols = pl.ds(c * 128, 128)
        ys_ref[:, cols] = w0 * ybuf[slot, pl.ds(0, tm), cols] + w1 * ybuf[slot, pl.ds(tm, tm), cols]
    gather(jnp.minimum(t + ahead, nt - 1), lax.rem(t + ahead, MOE_GATHER_DEPTH))
    xn = x_ref[...] + gate_ref[...] * ys_ref[...]
    if tail == "final":
        ms = jnp.mean(xn * xn, axis=-1, keepdims=True)
        o_ref[...] = xn * lax.rsqrt(ms + EPS) * g_ref[...]
    else:
        o_ref[...] = xn
    if tail == "codes":
        h = _norm_mod(xn, g_ref[...], sc_ref[...], sh_ref[...]).astype(BF16)
        a_ref[...] = jnp.dot(h, wa_ref[...], preferred_element_type=F32).astype(a_ref.dtype)

    @pl.when(t == nt - 1)
    def _():
        for k in range(1, MOE_GATHER_DEPTH):
            wait_rows(lax.rem(t + k, MOE_GATHER_DEPTH))


def _combine(yb, d0, d1, x, route, gate, seq, tail="plain", norm=None):
    n, d = x.shape
    tm = min(256, seq)
    row = lambda t, d0, d1: (t, 0)
    const = lambda t, d0, d1: (0, 0)
    per_batch = lambda t, d0, d1: (t * tm // seq, 0, 0)
    in_specs = [
        pl.BlockSpec(memory_space=pl.ANY),
        pl.BlockSpec((tm, d), row),
        pl.BlockSpec((tm, ROUTER_LANES), row),
        pl.BlockSpec((None, 1, d), per_batch),
    ]
    args = [yb, x, route, gate]
    out_specs = pl.BlockSpec((tm, d), row)
    out_shape = jax.ShapeDtypeStruct((n, d), F32)
    if tail == "codes":
        g, sc, sh, w_a = norm
        in_specs += [pl.BlockSpec((1, d), const), pl.BlockSpec((None, 1, d), per_batch),
                     pl.BlockSpec((None, 1, d), per_batch), pl.BlockSpec((d, 128), const)]
        args += [g.reshape(1, d), sc, sh, w_a]
        out_specs = [out_specs, pl.BlockSpec((tm, 128), row)]
        out_shape = [out_shape, jax.ShapeDtypeStruct((n, 128), BF16)]
    elif tail == "final":
        in_specs += [pl.BlockSpec((1, d), const)]
        args += [norm[0].reshape(1, d)]
    grid_spec = pltpu.PrefetchScalarGridSpec(
        num_scalar_prefetch=2,
        grid=(n // tm,),
        in_specs=in_specs,
        out_specs=out_specs,
        scratch_shapes=[pltpu.VMEM((MOE_GATHER_DEPTH, 2 * tm, d), F32),
                        pltpu.VMEM((tm, d), F32),
                        pltpu.SemaphoreType.DMA((MOE_GATHER_DEPTH,))],
    )
    return pl.pallas_call(
        functools.partial(_combine_kernel, tm=tm, tail=tail),
        grid_spec=grid_spec,
        out_shape=out_shape,
        compiler_params=_cparams(("arbitrary",)),
        name="moe_combine",
    )(d0, d1, *args)


def _proj_tile(nout):
    for tn in (1536, 896, 768, 512, 256, 128):
        if nout % tn == 0:
            return tn
    raise ValueError(f"no projection tile for width {nout}")


def kernel(x, c, ada_w, ada_b, norm1_g, norm2_g, na_w_in, na_w_out, na_rpb, gla_w_in, gla_w_gate_up, gla_b_gate, gla_gn_g, gla_w_out, moe_w_router_group, moe_w_router_expert, moe_w_up, moe_w_down, final_g):
    batch, seq, d = x.shape
    depth = ada_w.shape[0]
    n = batch * seq
    assert seq % (GRID_W * NA_ROWS_PER_STEP) == 0 and d % 1024 == 0
    dk = d // 2

    mod = _adaln(c, ada_w, ada_b)
    xf = x.reshape(n, d)
    i_na = 0
    i_gla = 0
    for i in range(depth):
        sh1, sc1, gt1, sh2, sc2, gt2 = [m.reshape(batch, 1, d) for m in jnp.split(mod[i], 6, axis=-1)]
        if i % 2 == 0:
            w_in = na_w_in[i_na].astype(BF16)
            colscale = jnp.concatenate([jnp.full((d,), NA_HEAD_DIM ** -0.5, F32), jnp.ones((2 * d,), F32)])
            qkv = _norm_mod_matmul(xf, norm1_g[i], sc1, sh1, w_in, colscale, seq, _proj_tile(3 * d))
            bias_tab = _na_bias_table(na_rpb[i_na])
            y = _na_attention(qkv, bias_tab, batch, seq, d)
            w_out = na_w_out[i_na].astype(BF16)
            i_na += 1
        else:
            hk = dk // GLA_HEADS
            w_in = gla_w_in[i_gla][:, :3 * d].astype(BF16)
            colscale = jnp.concatenate([jnp.full((dk,), hk ** -0.5, F32), jnp.ones((3 * d - dk,), F32)])
            proj = _norm_mod_matmul(xf, norm1_g[i], sc1, sh1, w_in, colscale, seq, _proj_tile(3 * d))
            wg_pad = jnp.zeros((2, 128, dk), F32)
            wg_pad = wg_pad.at[0, :GLA_GATE_RANK].set(gla_w_gate_up[i_gla, 0])
            wg_pad = wg_pad.at[1, GLA_GATE_RANK:2 * GLA_GATE_RANK].set(gla_w_gate_up[i_gla, 1])
            y = _gla(proj, a_low, wg_pad.astype(BF16), gla_b_gate[i_gla], gla_gn_g[i_gla], batch, seq, d)
            w_out = gla_w_out[i_gla].astype(BF16)
            i_gla += 1
        xf = _matmul_residual(y, w_out, xf, gt1, seq)

        w_router = jnp.concatenate(
            [moe_w_router_group[i], moe_w_router_expert[i],
             jnp.zeros((d, ROUTER_LANES - MOE_GROUPS - MOE_EXPERTS), F32)], axis=1).astype(BF16)
        h2, eid, route, cnt = _router(xf, norm2_g[i], sc2, sh2, w_router, seq)
        block_expert, nvalid, d0, d1, pad_lo, pad_hi, order, nxt = _dispatch_tables(eid, cnt)
        yb = _moe_experts(h2, block_expert, d0, d1, nvalid, pad_lo, pad_hi, order, nxt,
                          moe_w_up, moe_w_down, i)
        if i == depth - 1:
            xf = _combine(yb, d0, d1, xf, route, gt2, seq, "final", (final_g,))
        elif (i + 1) % 2 == 1:
            nsh1, nsc1 = [m.reshape(batch, 1, d) for m in jnp.split(mod[i + 1], 6, axis=-1)[:2]]
            w_a = jnp.concatenate([gla_w_in[i_gla][:, 3 * d:],
                                   jnp.zeros((d, 128 - 2 * GLA_GATE_RANK), F32)], axis=1).astype(BF16)
            xf, a_low = _combine(yb, d0, d1, xf, route, gt2, seq, "codes",
                                 (norm1_g[i + 1], nsc1, nsh1, w_a))
        else:
            xf = _combine(yb, d0, d1, xf, route, gt2, seq)

    return xf.reshape(batch, seq, d)
```
